```python
import math, functools
import jax, jax.numpy as jnp
from jax import lax
import numpy as np

D_MODEL = 2048
BATCH = 4
SEQ = 2048
DEPTH = 2
DEC_BATCH = 128
DEC_SEQ = 8
PAST_LEN = 2048
PAGE_SIZE = 128

N_BRANCH = 4
BRANCH_W = D_MODEL // 2
A_HEADS = 8
A_HD = BRANCH_W // (2 * A_HEADS)
ROPE_THETA = 10000.0
Q_BLOCK = 128
SUBLN_EPS = 1e-5
POOL_WINDOWS = (2, 4, 8, 16)
POOL_GW = BRANCH_W // len(POOL_WINDOWS)
POOL_BUF = max(POOL_WINDOWS) - 1
LRU_BLOCKS = 8
LRU_BW = BRANCH_W // LRU_BLOCKS
CONV_W = 4
LRU_C = 8.0
N_MEM = 256
M_HEADS = 4
M_HD = BRANCH_W // M_HEADS
N_IN = 10 * BRANCH_W + N_BRANCH * D_MODEL
IN_SPLITS = tuple(BRANCH_W * i for i in (1, 2, 3, 4, 5, 6, 10))
EPS = 1e-6
NEG = -1e30

kernel_name = 'hybrid_gated_diffattn_pool_rglru_mem_decoder_step'


def _f32(t):
    return t.astype(jnp.float32)


def _rmsnorm(x, g, eps=EPS):
    xf = _f32(x)
    y = xf * lax.rsqrt(jnp.mean(xf * xf, axis=-1, keepdims=True) + eps)
    return (y * _f32(g)).astype(x.dtype)


def _rope(x, pos):
    half = A_HD // 2
    inv = 1.0 / (ROPE_THETA ** (jnp.arange(half, dtype=jnp.float32) / half))
    ang = pos.astype(jnp.float32)[:, None] * inv[None, :]
    cos = jnp.cos(ang)[None, :, None, None, :]
    sin = jnp.sin(ang)[None, :, None, None, :]
    xf = _f32(x)
    x1, x2 = xf[..., :half], xf[..., half:]
    return jnp.concatenate([x1 * cos - x2 * sin, x2 * cos + x1 * sin], axis=-1).astype(x.dtype)


def _diff_probs(s, lam):
    p = jax.nn.softmax(s, axis=-1)
    return p[:, :, 0] - lam * p[:, :, 1]


def _attn_prompt(q, k, v, lam):
    B, S = q.shape[:2]
    nb = S // Q_BLOCK
    scale = A_HD ** -0.5
    qb = q.reshape(B, nb, Q_BLOCK, A_HEADS, 2, A_HD).transpose(1, 0, 2, 3, 4, 5)
    kpos = jnp.arange(S)

    def block(args):
        qi, bi = args
        s = _f32(jnp.einsum('bqhcd,bkhcd->bhcqk', qi, k)) * scale
        qpos = bi * Q_BLOCK + jnp.arange(Q_BLOCK)
        s = jnp.where(kpos[None, :] <= qpos[:, None], s, NEG)
        p = _diff_probs(s, lam)
        return jnp.einsum('bhqk,bkhe->bqhe', p.astype(v.dtype), v)

    o = lax.map(block, (qb, jnp.arange(nb)))
    return o.transpose(1, 0, 2, 3, 4).reshape(B, S, A_HEADS, 2 * A_HD)


def _attn_sample(q, k, v, lam, k_past, v_past):
    DB, T = q.shape[:2]
    scale = A_HD ** -0.5
    kp = k_past.reshape(DB, -1, A_HEADS, 2, A_HD)
    vp = v_past.reshape(DB, -1, A_HEADS, 2 * A_HD)
    n_past = kp.shape[1]
    s_past = _f32(jnp.einsum('bqhcd,bkhcd->bhcqk', q, kp)) * scale
    s_new = _f32(jnp.einsum('bqhcd,bkhcd->bhcqk', q, k)) * scale
    s_new = jnp.where(jnp.tril(jnp.ones((T, T), dtype=bool)), s_new, NEG)
    p = _diff_probs(jnp.concatenate([s_past, s_new], axis=-1), lam)
    p_past, p_new = p[..., :n_past], p[..., n_past:]
    return (jnp.einsum('bhqk,bkhe->bqhe', p_past.astype(vp.dtype), vp)
            + jnp.einsum('bhqk,bkhe->bqhe', p_new.astype(v.dtype), v))


def _pool_mixer(u, buf, pos, w, scale):
    B, T, _ = u.shape
    ext = jnp.concatenate([buf.astype(u.dtype), u], axis=1)
    extf = _f32(ext)
    cs = jnp.concatenate([jnp.zeros((B, 1, BRANCH_W), jnp.float32), jnp.cumsum(extf, axis=1)], axis=1)
    end = cs[:, POOL_BUF + 1:]
    cur = extf[:, POOL_BUF:]
    groups = []
    for gi, win in enumerate(POOL_WINDOWS):
        c0, c1 = gi * POOL_GW, (gi + 1) * POOL_GW
        start = cs[:, POOL_BUF + 1 - win:POOL_BUF + 1 - win + T, c0:c1]
        cnt = _f32(jnp.minimum(pos + 1, win))[None, :, None]
        groups.append((end[..., c0:c1] - start) / cnt - cur[..., c0:c1])
    mixed = jnp.stack(groups, axis=2).astype(u.dtype)
    out = jnp.einsum('btgc,gcd->btgd', mixed, w).reshape(B, T, BRANCH_W) * scale
    return out, ext[:, -POOL_BUF:]


def _causal_conv(xc, buf, w, b):
    T = xc.shape[1]
    ext = jnp.concatenate([buf.astype(xc.dtype), xc], axis=1)
    out = b + ext[:, CONV_W - 1:] * w[CONV_W - 1]
    for k in range(CONV_W - 1):
        out = out + ext[:, k:k + T] * w[k]
    return out, ext[:, -(CONV_W - 1):]


def _rglru(xc, h0, wa, ba, wx, bx, lam):
    B, T, _ = xc.shape
    xb = xc.reshape(B, T, LRU_BLOCKS, LRU_BW)
    r = jax.nn.sigmoid(jnp.einsum('btnc,ncd->btnd', xb, wa).reshape(B, T, BRANCH_W) + ba)
    i = jax.nn.sigmoid(jnp.einsum('btnc,ncd->btnd', xb, wx).reshape(B, T, BRANCH_W) + bx)
    log_a = -LRU_C * _f32(r) * jax.nn.softplus(-_f32(lam))
    a = jnp.exp(log_a)
    bterm = jnp.sqrt(-jnp.expm1(2.0 * log_a)) * _f32(i * xc)

    def step(h, ab):
        a_t, b_t = ab
        h = a_t * h + b_t
        return h, h

    h_last, hs = lax.scan(step, _f32(h0), (a.swapaxes(0, 1), bterm.swapaxes(0, 1)))
    return hs.swapaxes(0, 1).astype(xc.dtype), h_last


def _mem_kv(mem, g, w_k, w_v):
    B = mem.shape[0]
    mn = _rmsnorm(mem, g)
    k = (mn @ w_k).reshape(B, -1, M_HEADS, M_HD)
    v = (mn @ w_v).reshape(B, -1, M_HEADS, M_HD)
    return k, v


def _mem_attn(q, k, v):
    B, T = q.shape[:2]
    s = _f32(jnp.einsum('bthd,bmhd->bhtm', q, k)) * (M_HD ** -0.5)
    p = jax.nn.softmax(s, axis=-1)
    return jnp.einsum('bhtm,bmhd->bthd', p.astype(v.dtype), v).reshape(B, T, BRANCH_W)


def _layer(x, pos, layer_idx, attn_fn, mem_k, mem_v, pool_buf, conv_buf, h0,
           ln_g, w_in, lam_q1, lam_k1, lam_q2, lam_k2, attn_sub_g, pool_w, pool_scale,
           conv_w, conv_b, lru_wa, lru_ba, lru_wx, lru_bx, lru_lambda, w_branch, w_out):
    B, T, _ = x.shape
    xn = _rmsnorm(x, ln_g)
    proj = xn @ w_in
    q_a, k_a, v_a, u_b, x_c, q_d, z, g = jnp.split(proj, IN_SPLITS, axis=-1)
    q_a = _rope(q_a.reshape(B, T, A_HEADS, 2, A_HD), pos)
    k_a = _rope(k_a.reshape(B, T, A_HEADS, 2, A_HD), pos)
    v_a = v_a.reshape(B, T, A_HEADS, 2 * A_HD)
    lam_init = 0.8 - 0.6 * math.exp(-0.3 * layer_idx)
    lam = (jnp.exp(jnp.sum(_f32(lam_q1) * _f32(lam_k1)))
           - jnp.exp(jnp.sum(_f32(lam_q2) * _f32(lam_k2))) + lam_init)
    o_a = attn_fn(q_a, k_a, v_a, lam)
    o_a = (_rmsnorm(o_a, attn_sub_g, SUBLN_EPS) * (1.0 - lam_init)).reshape(B, T, BRANCH_W)
    o_b, pool_new = _pool_mixer(u_b, pool_buf, pos, pool_w, pool_scale)
    xc, conv_new = _causal_conv(x_c, conv_buf, conv_w, conv_b)
    o_c, h_new = _rglru(xc, h0, lru_wa, lru_ba, lru_wx, lru_bx, lru_lambda)
    o_d = _mem_attn(q_d.reshape(B, T, M_HEADS, M_HD), mem_k, mem_v)
    br = jnp.stack([o_a, o_b, o_c, o_d], axis=2)
    z = z.reshape(B, T, N_BRANCH, BRANCH_W)
    g = g.reshape(B, T, N_BRANCH, D_MODEL)
    up = jnp.einsum('btnw,nwd->btnd', br * jax.nn.silu(z), w_branch)
    merged = jnp.sum(jax.nn.sigmoid(g) * up, axis=2)
    x = x + merged @ w_out
    return x, k_a.reshape(B, T, 2 * A_HEADS, A_HD), v_a, pool_new, conv_new, h_new


def setup_inputs(seed: int = 0) -> dict:
    key = jax.random.key(seed)
    ks = iter(jax.random.split(key, 48))
    f32 = jnp.float32

    def nrm(shape, scale):
        return jax.random.normal(next(ks), shape, f32) * scale

    n_pages = PAST_LEN // PAGE_SIZE
    n_used = DEC_BATCH * n_pages
    n_pool = n_used + max(1, n_used // 4)
    page_table = jax.random.permutation(next(ks), n_pool)[:n_used].reshape(DEC_BATCH, n_pages).astype(jnp.int32)
    u = jax.random.uniform(next(ks), (DEPTH, BRANCH_W), f32, 0.9, 0.999)
    s = u ** (1.0 / LRU_C)
    lru_lambda = jnp.log(s) - jnp.log1p(-s)
    return {
        'x_prompt': nrm((BATCH, SEQ, D_MODEL), 1.0),
        'x_sample': nrm((DEC_BATCH, DEC_SEQ, D_MODEL), 1.0),
        'cache_k': nrm((DEPTH, n_pool, PAGE_SIZE, 2 * A_HEADS, A_HD), 1.0),
        'cache_v': nrm((DEPTH, n_pool, PAGE_SIZE, A_HEADS, 2 * A_HD), 1.0),
        'cache_mem_k': nrm((DEPTH, DEC_BATCH, N_MEM, M_HEADS, M_HD), 1.0),
        'cache_mem_v': nrm((DEPTH, DEC_BATCH, N_MEM, M_HEADS, M_HD), 1.0),
        'state_pool': nrm((DEPTH, DEC_BATCH, POOL_BUF, BRANCH_W), 1.0),
        'state_conv': nrm((DEPTH, DEC_BATCH, CONV_W - 1, BRANCH_W), 1.0),
        'state_h': nrm((DEPTH, DEC_BATCH, BRANCH_W), 0.5),
        'page_table': page_table,
        'mem_prompt': nrm((BATCH, N_MEM, D_MODEL), 1.0),
        'ln_g': 1.0 + nrm((DEPTH, D_MODEL), 0.02),
        'w_in': nrm((DEPTH, D_MODEL, N_IN), D_MODEL ** -0.5),
        'lam_q1': nrm((DEPTH, A_HD), 0.1),
        'lam_k1': nrm((DEPTH, A_HD), 0.1),
        'lam_q2': nrm((DEPTH, A_HD), 0.1),
        'lam_k2': nrm((DEPTH, A_HD), 0.1),
        'attn_sub_g': 1.0 + nrm((DEPTH, 2 * A_HD), 0.02),
        'pool_w': nrm((DEPTH, len(POOL_WINDOWS), POOL_GW, POOL_GW), POOL_GW ** -0.5),
        'pool_scale': 1.0 + nrm((DEPTH, BRANCH_W), 0.1),
        'conv_w': nrm((DEPTH, CONV_W, BRANCH_W), CONV_W ** -0.5),
        'conv_b': nrm((DEPTH, BRANCH_W), 0.01),
        'lru_wa': nrm((DEPTH, LRU_BLOCKS, LRU_BW, LRU_BW), LRU_BW ** -0.5),
        'lru_ba': nrm((DEPTH, BRANCH_W), 0.01),
        'lru_wx': nrm((DEPTH, LRU_BLOCKS, LRU_BW, LRU_BW), LRU_BW ** -0.5),
        'lru_bx': nrm((DEPTH, BRANCH_W), 0.01),
        'lru_lambda': lru_lambda,
        'mem_g': 1.0 + nrm((DEPTH, D_MODEL), 0.02),
        'w_mem_k': nrm((DEPTH, D_MODEL, BRANCH_W), D_MODEL ** -0.5),
        'w_mem_v': nrm((DEPTH, D_MODEL, BRANCH_W), D_MODEL ** -0.5),
        'w_branch': nrm((DEPTH, N_BRANCH, BRANCH_W, D_MODEL), BRANCH_W ** -0.5),
        'w_out': nrm((DEPTH, D_MODEL, D_MODEL), D_MODEL ** -0.5),
        'final_g': 1.0 + nrm((D_MODEL,), 0.02),
    }


def reference(x_prompt, x_sample, cache_k, cache_v, cache_mem_k, cache_mem_v, state_pool, state_conv,
              state_h, page_table, mem_prompt, ln_g, w_in, lam_q1, lam_k1, lam_q2, lam_k2, attn_sub_g,
              pool_w, pool_scale, conv_w, conv_b, lru_wa, lru_ba, lru_wx, lru_bx, lru_lambda, mem_g,
              w_mem_k, w_mem_v, w_branch, w_out, final_g):
    B, S, _ = x_prompt.shape
    DB, T, _ = x_sample.shape
    past_len = page_table.shape[1] * PAGE_SIZE
    pos_p = jnp.arange(S, dtype=jnp.int32)
    pos_s = past_len + jnp.arange(T, dtype=jnp.int32)
    zero_pool = jnp.zeros((B, POOL_BUF, BRANCH_W), x_prompt.dtype)
    zero_conv = jnp.zeros((B, CONV_W - 1, BRANCH_W), x_prompt.dtype)
    zero_h = jnp.zeros((B, BRANCH_W), jnp.float32)
    xp, xs = x_prompt, x_sample
    kp_l, vp_l, ks_l, vs_l, mk_l, mv_l = [], [], [], [], [], []
    pp_l, ps_l, cp_l, cs_l, hp_l, hs_l = [], [], [], [], [], []
    for l in range(DEPTH):
        lw = (ln_g[l], w_in[l], lam_q1[l], lam_k1[l], lam_q2[l], lam_k2[l], attn_sub_g[l], pool_w[l],
              pool_scale[l], conv_w[l], conv_b[l], lru_wa[l], lru_ba[l], lru_wx[l], lru_bx[l],
              lru_lambda[l], w_branch[l], w_out[l])
        mk, mv = _mem_kv(mem_prompt, mem_g[l], w_mem_k[l], w_mem_v[l])
        xp, k_new, v_new, p_new, c_new, h_new = _layer(
            xp, pos_p, l, _attn_prompt, mk, mv, zero_pool, zero_conv, zero_h, *lw)
        kp_l.append(k_new); vp_l.append(v_new); mk_l.append(mk); mv_l.append(mv)
        pp_l.append(p_new); cp_l.append(c_new); hp_l.append(h_new)
        attn_s = functools.partial(_attn_sample, k_past=cache_k[l, page_table], v_past=cache_v[l, page_table])
        xs, k_new, v_new, p_new, c_new, h_new = _layer(
            xs, pos_s, l, attn_s, cache_mem_k[l], cache_mem_v[l], state_pool[l], state_conv[l],
            state_h[l], *lw)
        ks_l.append(k_new); vs_l.append(v_new)
        ps_l.append(p_new); cs_l.append(c_new); hs_l.append(h_new)
    y_prompt = _rmsnorm(xp, final_g)
    y_sample = _rmsnorm(xs, final_g)
    return (y_prompt, y_sample,
            jnp.stack(kp_l), jnp.stack(vp_l), jnp.stack(ks_l), jnp.stack(vs_l),
            jnp.stack(mk_l), jnp.stack(mv_l),
            jnp.stack(pp_l), jnp.stack(ps_l), jnp.stack(cp_l), jnp.stack(cs_l),
            jnp.stack(hp_l), jnp.stack(hs_l))
```

```python
import functools
import math

import jax
import jax.numpy as jnp
from jax import lax
from jax.experimental import pallas as pl
from jax.experimental.pallas import tpu as pltpu

F32 = jnp.float32
BF16 = jnp.bfloat16

D_MODEL = 2048
BRANCH_W = 1024
A_HEADS = 8
A_HD = 64
HEAD_W = 2 * A_HD
ROPE_THETA = 10000.0
SUBLN_EPS = 1e-5
POOL_WINDOWS = (2, 4, 8, 16)
POOL_GW = BRANCH_W // len(POOL_WINDOWS)
POOL_BUF = max(POOL_WINDOWS) - 1
POOL_PAD = 16
LRU_BLOCKS = 8
LRU_BW = BRANCH_W // LRU_BLOCKS
CONV_W = 4
CONV_PAD = 8
LRU_C = 8.0
M_HEADS = 4
M_HD = BRANCH_W // M_HEADS
N_BRANCH = 4
PAGE_SIZE = 128
EPS = 1e-6
NEG = -1e30
SUBLANES = 8

COL_Q, COL_K, COL_V, COL_U, COL_X, COL_QD = 0, 1, 2, 3, 4, 5
COL_Z = 6
COL_G = 10

VMEM_LIMIT = 48 * 1024 * 1024


def _params(*sem):
    return pltpu.CompilerParams(dimension_semantics=sem, vmem_limit_bytes=VMEM_LIMIT)


def _silu(z):
    return z * jax.nn.sigmoid(z)


def _rmsnorm_kernel(x_ref, g_ref, o_ref, *, eps):
    x = x_ref[...]
    y = x * lax.rsqrt(jnp.mean(x * x, axis=-1, keepdims=True) + eps)
    o_ref[...] = (y * g_ref[...]).astype(o_ref.dtype)


def _rmsnorm(x, g, eps, out_dtype, tm=256):
    m, d = x.shape
    return pl.pallas_call(
        functools.partial(_rmsnorm_kernel, eps=eps),
        out_shape=jax.ShapeDtypeStruct((m, d), out_dtype),
        grid=(m // tm,),
        in_specs=[pl.BlockSpec((tm, d), lambda i: (i, 0)),
                  pl.BlockSpec((1, d), lambda i: (0, 0))],
        out_specs=pl.BlockSpec((tm, d), lambda i: (i, 0)),
        compiler_params=_params("parallel"),
        name="rmsnorm",
    )(x, g.reshape(1, d))


def _mm_kernel(a_ref, w_ref, *rest, has_res):
    if has_res:
        r_ref, o_ref, wb_ref = rest
    else:
        o_ref, wb_ref = rest

    @pl.when(pl.program_id(1) == 0)
    def _():
        wb_ref[...] = w_ref[...].astype(BF16)

    acc = jnp.dot(a_ref[...], wb_ref[...], preferred_element_type=F32)
    if has_res:
        acc = r_ref[...] + acc
    o_ref[...] = acc


def _matmul(a, w, res=None, tm=512, tn=1024):
    m, k = a.shape
    n = w.shape[1]
    tm = min(tm, m)
    in_specs = [pl.BlockSpec((tm, k), lambda j, i: (i, 0)),
                pl.BlockSpec((k, tn), lambda j, i: (0, j))]
    args = [a, w]
    if res is not None:
        in_specs.append(pl.BlockSpec((tm, tn), lambda j, i: (i, j)))
        args.append(res)
    return pl.pallas_call(
        functools.partial(_mm_kernel, has_res=res is not None),
        out_shape=jax.ShapeDtypeStruct((m, n), F32),
        grid=(n // tn, m // tm),
        in_specs=in_specs,
        out_specs=pl.BlockSpec((tm, tn), lambda j, i: (i, j)),
        scratch_shapes=[pltpu.VMEM((k, tn), BF16)],
        compiler_params=_params("arbitrary", "arbitrary"),
        name="matmul",
    )(*args)


def _rope(x, cos, sin_signed):
    lane = lax.broadcasted_iota(jnp.int32, x.shape, 1)
    first_half = (lane % A_HD) < (A_HD // 2)
    width = x.shape[1]
    partner = jnp.where(first_half,
                        pltpu.roll(x, width - A_HD // 2, 1),
                        pltpu.roll(x, A_HD // 2, 1))
    reps = width // HEAD_W
    return x * jnp.tile(cos, (1, reps)) + partner * jnp.tile(sin_signed, (1, reps))


def _rope_kernel(q_ref, k_ref, v_ref, cos_ref, sin_ref,
                 ko_ref, vo_ref, kb_ref, vb_ref, qo_ref, *, stacked_q):
    cos = cos_ref[...]
    sin = sin_ref[...]
    k = _rope(k_ref[...], cos, sin)
    ko_ref[...] = k
    kb_ref[...] = k.astype(BF16)
    v = v_ref[...]
    vo_ref[...] = v
    vb_ref[...] = v.astype(BF16)
    q = _rope(q_ref[...], cos, sin) * (A_HD ** -0.5)
    if stacked_q:
        tm = q.shape[0]
        lane = lax.broadcasted_iota(jnp.int32, (tm, HEAD_W), 1)
        for h in range(A_HEADS):
            qh = q[:, h * HEAD_W:(h + 1) * HEAD_W]
            qo_ref[h, 0] = jnp.where(lane < A_HD, qh, 0.0).astype(BF16)
            qo_ref[h, 1] = jnp.where(lane >= A_HD, qh, 0.0).astype(BF16)
    else:
        qo_ref[...] = q.astype(BF16)


def _rope_call(proj, cos, sin_signed, *, seq_len, tm):
    m = proj.shape[0]
    stacked = seq_len > 0
    if stacked:
        tps = seq_len // tm
        tab_map = lambda i: (i % tps, 0)
        q_shape = jax.ShapeDtypeStruct((m // seq_len, A_HEADS, 2, seq_len, HEAD_W), BF16)
        q_spec = pl.BlockSpec((None, A_HEADS, 2, tm, HEAD_W),
                              lambda i: (i // tps, 0, 0, i % tps, 0))
    else:
        tab_map = lambda i: (0, 0)
        q_shape = jax.ShapeDtypeStruct((m, BRANCH_W), BF16)
        q_spec = pl.BlockSpec((tm, BRANCH_W), lambda i: (i, 0))
    col = lambda c: pl.BlockSpec((tm, BRANCH_W), lambda i, c=c: (i, c))
    row = pl.BlockSpec((tm, BRANCH_W), lambda i: (i, 0))
    return pl.pallas_call(
        functools.partial(_rope_kernel, stacked_q=stacked),
        out_shape=(jax.ShapeDtypeStruct((m, BRANCH_W), F32),
                   jax.ShapeDtypeStruct((m, BRANCH_W), F32),
                   jax.ShapeDtypeStruct((m, BRANCH_W), BF16),
                   jax.ShapeDtypeStruct((m, BRANCH_W), BF16),
                   q_shape),
        grid=(m // tm,),
        in_specs=[col(COL_Q), col(COL_K), col(COL_V),
                  pl.BlockSpec((tm, HEAD_W), tab_map),
                  pl.BlockSpec((tm, HEAD_W), tab_map)],
        out_specs=(row, row, row, row, q_spec),
        compiler_params=_params("parallel"),
        name="rope",
    )(proj, proj, proj, cos, sin_signed)


def _rope_tables(pos):
    half = A_HD // 2
    inv = 1.0 / (ROPE_THETA ** (jnp.arange(half, dtype=F32) / half))
    ang = pos.astype(F32)[:, None] * inv[None, :]
    cos, sin = jnp.cos(ang), jnp.sin(ang)
    cos = jnp.concatenate([cos, cos], axis=-1)
    sin = jnp.concatenate([-sin, sin], axis=-1)
    return jnp.tile(cos, (1, 2)), jnp.tile(sin, (1, 2))


def _lambda(lam_ref, lam_init):
    lv = lam_ref[...]
    l1 = jnp.sum(lv[0:1] * lv[1:2], axis=-1, keepdims=True)
    l2 = jnp.sum(lv[2:3] * lv[3:4], axis=-1, keepdims=True)
    return jnp.exp(l1) - jnp.exp(l2) + lam_init


def _diff_head_out(o1, o2, lam, subg, z, lam_init):
    d = o1 - lam * o2
    y = d * lax.rsqrt(jnp.mean(d * d, axis=-1, keepdims=True) + SUBLN_EPS) * subg
    return (y * (1.0 - lam_init)) * _silu(z)


def _online_softmax_step(s, v, m_ref, l_ref, acc_ref, pv_fn):
    m_prev = m_ref[...]
    m_new = jnp.maximum(m_prev, jnp.max(s, axis=-1, keepdims=True))
    alpha = jnp.exp(m_prev - m_new)
    p = jnp.exp(s - m_new)
    l_ref[...] = alpha * l_ref[...] + jnp.sum(p, axis=-1, keepdims=True)
    acc_ref[...] = alpha * acc_ref[...] + pv_fn(p.astype(BF16), v)
    m_ref[...] = m_new


_NT = (((1,), (1,)), ((), ()))


def _attn_p_kernel(q_ref, k_ref, v_ref, z_ref, subg_ref, lam_ref, o_ref,
                   m_ref, l_ref, acc_ref, *, tq, lam_init):
    qi = pl.program_id(2)
    q = q_ref[...].reshape(2 * tq, HEAD_W)
    m_ref[...] = jnp.full(m_ref.shape, NEG, F32)
    l_ref[...] = jnp.zeros(l_ref.shape, F32)
    acc_ref[...] = jnp.zeros(acc_ref.shape, F32)

    def block(start, masked):
        k = k_ref[pl.ds(start, tq), :]
        v = v_ref[pl.ds(start, tq), :]
        s = lax.dot_general(q, k, _NT, preferred_element_type=F32)
        if masked:
            r = lax.broadcasted_iota(jnp.int32, s.shape, 0)
            c = lax.broadcasted_iota(jnp.int32, s.shape, 1)
            s = jnp.where(c <= (r % tq), s, NEG)
        _online_softmax_step(s, v, m_ref, l_ref, acc_ref,
                             lambda p, vv: jnp.dot(p, vv, preferred_element_type=F32))

    def body(ki, carry):
        block(pl.multiple_of(ki * tq, tq), False)
        return carry

    lax.fori_loop(0, qi, body, 0)
    block(pl.multiple_of(qi * tq, tq), True)

    o = acc_ref[...] / l_ref[...]
    lam = _lambda(lam_ref, lam_init)
    out = _diff_head_out(o[:tq], o[tq:], lam, subg_ref[...], z_ref[...], lam_init)
    o_ref[...] = out.astype(o_ref.dtype)


def _attn_prompt(q_st, k_bf, v_bf, proj, subg, lam_vecs, lam_init, *, batch, seq_len, tq=512):
    nq = seq_len // tq
    zcol = COL_Z * (BRANCH_W // HEAD_W)
    kv_spec = pl.BlockSpec((seq_len, HEAD_W), lambda b, h, qi: (b, h))
    return pl.pallas_call(
        functools.partial(_attn_p_kernel, tq=tq, lam_init=lam_init),
        out_shape=jax.ShapeDtypeStruct((batch * seq_len, BRANCH_W), BF16),
        grid=(batch, A_HEADS, nq),
        in_specs=[pl.BlockSpec((None, None, 2, tq, HEAD_W), lambda b, h, qi: (b, h, 0, qi, 0)),
                  kv_spec, kv_spec,
                  pl.BlockSpec((tq, HEAD_W), lambda b, h, qi: (b * nq + qi, zcol + h)),
                  pl.BlockSpec((1, HEAD_W), lambda b, h, qi: (0, 0)),
                  pl.BlockSpec((4, A_HD), lambda b, h, qi: (0, 0))],
        out_specs=pl.BlockSpec((tq, HEAD_W), lambda b, h, qi: (b * nq + qi, h)),
        scratch_shapes=[pltpu.VMEM((2 * tq, 1), F32), pltpu.VMEM((2 * tq, 1), F32),
                        pltpu.VMEM((2 * tq, HEAD_W), F32)],
        compiler_params=_params("parallel", "parallel", "arbitrary"),
        name="attn_prompt",
    )(q_st, k_bf, v_bf, proj, subg, lam_vecs)


def _attn_s_kernel(pt_ref, q_ref, kn_ref, vn_ref, kc_ref, vc_ref, z_ref, subg_ref, lam_ref,
                   o_ref, qbd_ref, knp_ref, vnp_ref, m_ref, l_ref, acc_ref, *, t, lam_init):
    del pt_ref
    p_idx = pl.program_id(1)
    hrows = 2 * t

    def update(s, v_head):
        m_prev = m_ref[...]
        m_new = jnp.maximum(m_prev, jnp.max(s, axis=-1, keepdims=True))
        alpha = jnp.exp(m_prev - m_new)
        p = jnp.exp(s - m_new)
        l_ref[...] = alpha * l_ref[...] + jnp.sum(p, axis=-1, keepdims=True)
        pb = p.astype(BF16)
        pv = jnp.concatenate(
            [jnp.dot(pb[h * hrows:(h + 1) * hrows], v_head(h), preferred_element_type=F32)
             for h in range(A_HEADS)], axis=0)
        acc_ref[...] = alpha * acc_ref[...] + pv
        m_ref[...] = m_new

    @pl.when(p_idx == 0)
    def _():
        q = q_ref[...]
        lane = lax.broadcasted_iota(jnp.int32, q.shape, 1)
        for hc in range(2 * A_HEADS):
            keep = (lane >= hc * A_HD) & (lane < (hc + 1) * A_HD)
            qbd_ref[hc * t:(hc + 1) * t, :] = jnp.where(keep, q, jnp.zeros_like(q))
        m_ref[...] = jnp.full(m_ref.shape, NEG, F32)
        l_ref[...] = jnp.zeros(l_ref.shape, F32)
        acc_ref[...] = jnp.zeros(acc_ref.shape, F32)
        knp_ref[...] = jnp.zeros(knp_ref.shape, BF16)
        vnp_ref[...] = jnp.zeros(vnp_ref.shape, BF16)
        knp_ref[0:t, :] = kn_ref[...]
        vnp_ref[0:t, :] = vn_ref[...]
        s = lax.dot_general(qbd_ref[...], knp_ref[...], _NT, preferred_element_type=F32)
        r = lax.broadcasted_iota(jnp.int32, s.shape, 0)
        c = lax.broadcasted_iota(jnp.int32, s.shape, 1)
        s = jnp.where(c <= (r % t), s, NEG)
        update(s, lambda h: vnp_ref[:, h * HEAD_W:(h + 1) * HEAD_W])

    kc = kc_ref[...].reshape(BRANCH_W, PAGE_SIZE).astype(BF16)
    s = jnp.dot(qbd_ref[...], kc, preferred_element_type=F32)
    update(s, lambda h: vc_ref[:, h, :].astype(BF16))

    @pl.when(p_idx == pl.num_programs(1) - 1)
    def _():
        o = acc_ref[...] / l_ref[...]
        lam = _lambda(lam_ref, lam_init)
        subg = subg_ref[...]
        for h in range(A_HEADS):
            o1 = o[(2 * h) * t:(2 * h + 1) * t]
            o2 = o[(2 * h + 1) * t:(2 * h + 2) * t]
            zh = z_ref[:, h * HEAD_W:(h + 1) * HEAD_W]
            out = _diff_head_out(o1, o2, lam, subg, zh, lam_init)
            o_ref[:, h * HEAD_W:(h + 1) * HEAD_W] = out.astype(o_ref.dtype)


def _attn_sample(q_bf, k_bf, v_bf, cache_k, cache_v, layer, page_table, proj, subg, lam_vecs,
                 lam_init, *, n_seq, t):
    n_pages = page_table.shape[1]
    rows = 2 * A_HEADS * t
    pt_flat = page_table.reshape(-1)
    kc = jnp.transpose(cache_k, (0, 1, 3, 4, 2))
    seq_spec = pl.BlockSpec((t, BRANCH_W), lambda b, p, pt: (b, 0))
    page = lambda b, p, pt: (layer, pt[b * n_pages + p], 0, 0, 0)
    grid_spec = pltpu.PrefetchScalarGridSpec(
        num_scalar_prefetch=1,
        grid=(n_seq, n_pages),
        in_specs=[seq_spec, seq_spec, seq_spec,
                  pl.BlockSpec((None, None, 2 * A_HEADS, A_HD, PAGE_SIZE), page),
                  pl.BlockSpec((None, None, PAGE_SIZE, A_HEADS, HEAD_W), page),
                  pl.BlockSpec((t, BRANCH_W), lambda b, p, pt: (b, COL_Z)),
                  pl.BlockSpec((1, HEAD_W), lambda b, p, pt: (0, 0)),
                  pl.BlockSpec((4, A_HD), lambda b, p, pt: (0, 0))],
        out_specs=pl.BlockSpec((t, BRANCH_W), lambda b, p, pt: (b, 0)),
        scratch_shapes=[pltpu.VMEM((rows, BRANCH_W), BF16),
                        pltpu.VMEM((PAGE_SIZE, BRANCH_W), BF16),
                        pltpu.VMEM((PAGE_SIZE, BRANCH_W), BF16),
                        pltpu.VMEM((rows, 1), F32), pltpu.VMEM((rows, 1), F32),
                        pltpu.VMEM((rows, HEAD_W), F32)],
    )
    return pl.pallas_call(
        functools.partial(_attn_s_kernel, t=t, lam_init=lam_init),
        out_shape=jax.ShapeDtypeStruct((n_seq * t, BRANCH_W), BF16),
        grid_spec=grid_spec,
        compiler_params=_params("parallel", "arbitrary"),
        name="attn_sample",
    )(pt_flat, q_bf, k_bf, v_bf, kc, cache_v, proj, subg, lam_vecs)


def _window_sum(e, log2_win):
    s = e
    for k in range(log2_win):
        s = s + pltpu.roll(s, 1 << k, 0)
    return s


def _pool_groups(ext, pad, pos, w_ref, scale_ref, z, o_ref, pick):
    del pad
    for gi, win in enumerate(POOL_WINDOWS):
        c0, c1 = gi * POOL_GW, (gi + 1) * POOL_GW
        e = ext[:, c0:c1]
        cur = pick(e)
        tot = pick(_window_sum(e, gi + 1))
        cnt = jnp.minimum(pos + 1, win).astype(F32)
        mixed = tot / cnt - cur
        out = jnp.dot(mixed.astype(BF16), w_ref[gi].astype(BF16), preferred_element_type=F32)
        out = out * scale_ref[:, c0:c1]
        o_ref[:, c0:c1] = (out * _silu(z[:, c0:c1])).astype(o_ref.dtype)


def _pool_p_kernel(u_ref, prev_ref, z_ref, w_ref, scale_ref, o_ref, *, tm, tps):
    i = pl.program_id(0)
    first = (i % tps) == 0
    prev = jnp.where(first, 0.0, prev_ref[...])
    ext = jnp.concatenate([prev, u_ref[...]], axis=0)
    pos = (i % tps) * tm + lax.broadcasted_iota(jnp.int32, (tm, POOL_GW), 0)
    _pool_groups(ext, POOL_PAD, pos, w_ref, scale_ref, z_ref[...], o_ref,
                 lambda x: x[POOL_PAD:])


def _pool_s_kernel(u_ref, buf_ref, z_ref, w_ref, scale_ref, o_ref, *, nb, t, past_len):
    u3 = u_ref[...].reshape(nb, t, BRANCH_W)
    ext = jnp.concatenate([buf_ref[...], u3], axis=1)
    ext = ext.reshape(nb * (POOL_PAD + t), BRANCH_W)
    pos = past_len + lax.broadcasted_iota(jnp.int32, (nb * t, POOL_GW), 0) % t

    def pick(x):
        x3 = x.reshape(nb, POOL_PAD + t, x.shape[-1])
        return x3[:, POOL_PAD:].reshape(nb * t, x.shape[-1])

    _pool_groups(ext, POOL_PAD, pos, w_ref, scale_ref, z_ref[...], o_ref, pick)


def _pool_prompt(proj, pool_w, pool_scale, *, seq_len, tm=256):
    m = proj.shape[0]
    tps = seq_len // tm
    per = tm // POOL_PAD
    return pl.pallas_call(
        functools.partial(_pool_p_kernel, tm=tm, tps=tps),
        out_shape=jax.ShapeDtypeStruct((m, BRANCH_W), BF16),
        grid=(m // tm,),
        in_specs=[pl.BlockSpec((tm, BRANCH_W), lambda i: (i, COL_U)),
                  pl.BlockSpec((POOL_PAD, BRANCH_W),
                               lambda i: (jnp.maximum(i * per - 1, 0), COL_U)),
                  pl.BlockSpec((tm, BRANCH_W), lambda i: (i, COL_Z + 1)),
                  pl.BlockSpec(pool_w.shape, lambda i: (0, 0, 0)),
                  pl.BlockSpec((1, BRANCH_W), lambda i: (0, 0))],
        out_specs=pl.BlockSpec((tm, BRANCH_W), lambda i: (i, 0)),
        compiler_params=_params("parallel"),
        name="pool_prompt",
    )(proj, proj, proj, pool_w, pool_scale.reshape(1, BRANCH_W))


def _pool_sample(proj, buf_padded, pool_w, pool_scale, *, t, past_len, nb=32):
    m = proj.shape[0]
    return pl.pallas_call(
        functools.partial(_pool_s_kernel, nb=nb, t=t, past_len=past_len),
        out_shape=jax.ShapeDtypeStruct((m, BRANCH_W), BF16),
        grid=(m // (nb * t),),
        in_specs=[pl.BlockSpec((nb * t, BRANCH_W), lambda i: (i, COL_U)),
                  pl.BlockSpec((nb, POOL_PAD, BRANCH_W), lambda i: (i, 0, 0)),
                  pl.BlockSpec((nb * t, BRANCH_W), lambda i: (i, COL_Z + 1)),
                  pl.BlockSpec(pool_w.shape, lambda i: (0, 0, 0)),
                  pl.BlockSpec((1, BRANCH_W), lambda i: (0, 0))],
        out_specs=pl.BlockSpec((nb * t, BRANCH_W), lambda i: (i, 0)),
        compiler_params=_params("parallel"),
        name="pool_sample",
    )(proj, buf_padded, proj, pool_w, pool_scale.reshape(1, BRANCH_W))


def _conv_taps(ext, pick, cw_ref, cb_ref):
    out = cb_ref[...] + pick(ext) * cw_ref[CONV_W - 1:CONV_W, :]
    for k in range(CONV_W - 1):
        shift = CONV_W - 1 - k
        out = out + pick(pltpu.roll(ext, shift, 0)) * cw_ref[k:k + 1, :]
    return out


def _block_diag_gate(xb, w_ref, b_ref):
    parts = [jnp.dot(xb[:, n * LRU_BW:(n + 1) * LRU_BW], w_ref[n].astype(BF16),
                     preferred_element_type=F32) for n in range(LRU_BLOCKS)]
    return jax.nn.sigmoid(jnp.concatenate(parts, axis=-1) + b_ref[...])


def _lru_coeffs(xc, wa_ref, ba_ref, wx_ref, bx_ref, lam_ref):
    xb = xc.astype(BF16)
    r = _block_diag_gate(xb, wa_ref, ba_ref)
    i = _block_diag_gate(xb, wx_ref, bx_ref)
    nl = -lam_ref[...]
    softplus = jnp.maximum(nl, 0.0) + jnp.log1p(jnp.exp(-jnp.abs(nl)))
    log_a = -LRU_C * r * softplus
    a = jnp.exp(log_a)
    b = jnp.sqrt(1.0 - jnp.exp(2.0 * log_a)) * (i * xc)
    return a, b


def _group_scan(a, b):
    row = lax.broadcasted_iota(jnp.int32, a.shape, 0) % SUBLANES
    d = 1
    while d < SUBLANES:
        valid = row >= d
        a_sh = pltpu.roll(a, d, 0)
        b_sh = pltpu.roll(b, d, 0)
        b = jnp.where(valid, a * b_sh + b, b)
        a = jnp.where(valid, a * a_sh, a)
        d *= 2
    return a, b


def _lru_p_kernel(x_ref, prev_ref, z_ref, cw_ref, cb_ref, wa_ref, ba_ref, wx_ref, bx_ref,
                  lam_ref, o_ref, hl_ref, a_s, b_s, h_s, *, tm, tps):
    i = pl.program_id(0)
    first = (i % tps) == 0

    @pl.when(first)
    def _():
        h_s[...] = jnp.zeros(h_s.shape, F32)

    prev = jnp.where(first, 0.0, prev_ref[...])
    ext = jnp.concatenate([prev, x_ref[...]], axis=0)
    xc = _conv_taps(ext, lambda x: x[CONV_PAD:], cw_ref, cb_ref)
    a, b = _lru_coeffs(xc, wa_ref, ba_ref, wx_ref, bx_ref, lam_ref)
    a, b = _group_scan(a, b)
    a_s[...] = a
    b_s[...] = b

    def body(g, h):
        r0 = pl.multiple_of(g * SUBLANES, SUBLANES)
        hs = a_s[pl.ds(r0, SUBLANES), :] * h + b_s[pl.ds(r0, SUBLANES), :]
        b_s[pl.ds(r0, SUBLANES), :] = hs
        return hs[SUBLANES - 1:SUBLANES, :]

    h = lax.fori_loop(0, tm // SUBLANES, body, h_s[...])
    h_s[...] = h
    hl_ref[...] = h
    o_ref[...] = (b_s[...] * _silu(z_ref[...])).astype(o_ref.dtype)


def _lru_s_kernel(x_ref, buf_ref, h0_ref, z_ref, cw_ref, cb_ref, wa_ref, ba_ref, wx_ref, bx_ref,
                  lam_ref, o_ref, hs_ref, *, nb, t):
    x3 = x_ref[...].reshape(nb, t, BRANCH_W)
    ext = jnp.concatenate([buf_ref[...], x3], axis=1).reshape(nb * (CONV_PAD + t), BRANCH_W)

    def pick(x):
        return x.reshape(nb, CONV_PAD + t, BRANCH_W)[:, CONV_PAD:].reshape(nb * t, BRANCH_W)

    xc = _conv_taps(ext, pick, cw_ref, cb_ref)
    a, b = _lru_coeffs(xc, wa_ref, ba_ref, wx_ref, bx_ref, lam_ref)
    a, b = _group_scan(a, b)
    hs = a * h0_ref[...] + b
    hs_ref[...] = hs
    o_ref[...] = (hs * _silu(z_ref[...])).astype(o_ref.dtype)


def _lru_weight_specs(nargs_map):
    full = lambda shape: pl.BlockSpec(shape, lambda *a: (0,) * len(shape))
    del nargs_map
    return [full((CONV_W, BRANCH_W)), full((1, BRANCH_W)),
            full((LRU_BLOCKS, LRU_BW, LRU_BW)), full((1, BRANCH_W)),
            full((LRU_BLOCKS, LRU_BW, LRU_BW)), full((1, BRANCH_W)),
            full((1, BRANCH_W))]


def _lru_weights(conv_w, conv_b, wa, ba, wx, bx, lam):
    r = lambda v: v.reshape(1, BRANCH_W)
    return (conv_w, r(conv_b), wa, r(ba), wx, r(bx), r(lam))


def _lru_prompt(proj, weights, *, batch, seq_len, tm=256):
    m = proj.shape[0]
    tps = seq_len // tm
    per = tm // CONV_PAD
    return pl.pallas_call(
        functools.partial(_lru_p_kernel, tm=tm, tps=tps),
        out_shape=(jax.ShapeDtypeStruct((m, BRANCH_W), BF16),
                   jax.ShapeDtypeStruct((batch, 1, BRANCH_W), F32)),
        grid=(m // tm,),
        in_specs=[pl.BlockSpec((tm, BRANCH_W), lambda i: (i, COL_X)),
                  pl.BlockSpec((CONV_PAD, BRANCH_W),
                               lambda i: (jnp.maximum(i * per - 1, 0), COL_X)),
                  pl.BlockSpec((tm, BRANCH_W), lambda i: (i, COL_Z + 2))]
        + _lru_weight_specs(None),
        out_specs=(pl.BlockSpec((tm, BRANCH_W), lambda i: (i, 0)),
                   pl.BlockSpec((None, 1, BRANCH_W), lambda i: (i // tps, 0, 0))),
        scratch_shapes=[pltpu.VMEM((tm, BRANCH_W), F32), pltpu.VMEM((tm, BRANCH_W), F32),
                        pltpu.VMEM((1, BRANCH_W), F32)],
        compiler_params=_params("arbitrary"),
        name="lru_prompt",
    )(proj, proj, proj, *weights)


def _lru_sample(proj, buf_padded, h0_rows, weights, *, t, nb=32):
    m = proj.shape[0]
    rows = nb * t
    return pl.pallas_call(
        functools.partial(_lru_s_kernel, nb=nb, t=t),
        out_shape=(jax.ShapeDtypeStruct((m, BRANCH_W), BF16),
                   jax.ShapeDtypeStruct((m, BRANCH_W), F32)),
        grid=(m // rows,),
        in_specs=[pl.BlockSpec((rows, BRANCH_W), lambda i: (i, COL_X)),
                  pl.BlockSpec((nb, CONV_PAD, BRANCH_W), lambda i: (i, 0, 0)),
                  pl.BlockSpec((rows, BRANCH_W), lambda i: (i, 0)),
                  pl.BlockSpec((rows, BRANCH_W), lambda i: (i, COL_Z + 2))]
        + _lru_weight_specs(None),
        out_specs=(pl.BlockSpec((rows, BRANCH_W), lambda i: (i, 0)),
                   pl.BlockSpec((rows, BRANCH_W), lambda i: (i, 0))),
        compiler_params=_params("parallel"),
        name="lru_sample",
    )(proj, buf_padded, h0_rows, proj, *weights)


def _mem_attn_kernel(q_ref, k_ref, v_ref, z_ref, o_ref):
    q = q_ref[...].astype(BF16)
    k = k_ref[...].astype(BF16)
    v = v_ref[...].astype(BF16)
    z = z_ref[...]
    for h in range(M_HEADS):
        c0, c1 = h * M_HD, (h + 1) * M_HD
        s = lax.dot_general(q[:, c0:c1], k[:, c0:c1], _NT, preferred_element_type=F32)
        s = s * (M_HD ** -0.5)
        p = jnp.exp(s - jnp.max(s, axis=-1, keepdims=True))
        den = jnp.sum(p, axis=-1, keepdims=True)
        o = jnp.dot((p / den).astype(BF16), v[:, c0:c1], preferred_element_type=F32)
        o_ref[:, c0:c1] = (o * _silu(z[:, c0:c1])).astype(o_ref.dtype)


def _mem_attn(proj, mem_k, mem_v, *, n_seq, rows_per_seq, tq):
    n_mem = mem_k.shape[0] // n_seq
    nq = rows_per_seq // tq
    kv_spec = pl.BlockSpec((n_mem, BRANCH_W), lambda b, qi: (b, 0))
    return pl.pallas_call(
        _mem_attn_kernel,
        out_shape=jax.ShapeDtypeStruct((n_seq * rows_per_seq, BRANCH_W), BF16),
        grid=(n_seq, nq),
        in_specs=[pl.BlockSpec((tq, BRANCH_W), lambda b, qi: (b * nq + qi, COL_QD)),
                  kv_spec, kv_spec,
                  pl.BlockSpec((tq, BRANCH_W), lambda b, qi: (b * nq + qi, COL_Z + 3))],
        out_specs=pl.BlockSpec((tq, BRANCH_W), lambda b, qi: (b * nq + qi, 0)),
        compiler_params=_params("parallel", "parallel"),
        name="mem_attn",
    )(proj, mem_k, mem_v, proj)


def _merge_kernel(ha_ref, hb_ref, hc_ref, hd_ref, g_ref, w_ref, o_ref, acc_ref):
    n = pl.program_id(2)

    @pl.when(n == 0)
    def _():
        acc_ref[...] = jnp.zeros(acc_ref.shape, F32)

    wb = w_ref[...].astype(BF16)
    gate = jax.nn.sigmoid(g_ref[...])
    for idx, h_ref in enumerate((ha_ref, hb_ref, hc_ref, hd_ref)):
        @pl.when(n == idx)
        def _(h_ref=h_ref):
            acc_ref[...] += gate * jnp.dot(h_ref[...], wb, preferred_element_type=F32)

    @pl.when(n == N_BRANCH - 1)
    def _():
        o_ref[...] = acc_ref[...].astype(o_ref.dtype)


def _merge(h_a, h_b, h_c, h_d, proj, w_branch, tm=512, tn=1024):
    m = h_a.shape[0]
    tm = min(tm, m)
    nj = D_MODEL // tn
    h_spec = pl.BlockSpec((tm, BRANCH_W), lambda i, j, n: (i, 0))
    return pl.pallas_call(
        _merge_kernel,
        out_shape=jax.ShapeDtypeStruct((m, D_MODEL), BF16),
        grid=(m // tm, nj, N_BRANCH),
        in_specs=[h_spec, h_spec, h_spec, h_spec,
                  pl.BlockSpec((tm, tn), lambda i, j, n: (i, COL_G * (BRANCH_W // tn) + n * nj + j)),
                  pl.BlockSpec((None, BRANCH_W, tn), lambda i, j, n: (n, 0, j))],
        out_specs=pl.BlockSpec((tm, tn), lambda i, j, n: (i, j)),
        scratch_shapes=[pltpu.VMEM((tm, tn), F32)],
        compiler_params=_params("parallel", "parallel", "arbitrary"),
        name="merge",
    )(h_a, h_b, h_c, h_d, proj, w_branch)


def _layer_dense_in(x, ln_g, w_in):
    xn = _rmsnorm(x, ln_g, EPS, BF16)
    return _matmul(xn, w_in)


def _layer_dense_out(x, h_a, h_b, h_c, h_d, proj, w_branch, w_out):
    merged = _merge(h_a, h_b, h_c, h_d, proj, w_branch)
    return _matmul(merged, w_out, res=x)


def kernel(x_prompt, x_sample, cache_k, cache_v, cache_mem_k, cache_mem_v, state_pool, state_conv, state_h, page_table, mem_prompt, ln_g, w_in, lam_q1, lam_k1, lam_q2, lam_k2, attn_sub_g, pool_w, pool_scale, conv_w, conv_b, lru_wa, lru_ba, lru_wx, lru_bx, lru_lambda, mem_g, w_mem_k, w_mem_v, w_branch, w_out, final_g):
    batch, seq_len, _ = x_prompt.shape
    n_seq, t, _ = x_sample.shape
    depth = w_in.shape[0]
    n_mem = mem_prompt.shape[1]
    past_len = page_table.shape[1] * PAGE_SIZE
    rope_tm = 256

    cos_p, sin_p = _rope_tables(jnp.arange(seq_len, dtype=jnp.int32))
    cos_s, sin_s = _rope_tables(past_len + jnp.arange(t, dtype=jnp.int32))
    cos_s = jnp.tile(cos_s, (rope_tm // t, 1))
    sin_s = jnp.tile(sin_s, (rope_tm // t, 1))

    xp = x_prompt.reshape(batch * seq_len, D_MODEL)
    xs = x_sample.reshape(n_seq * t, D_MODEL)
    mem_rows = mem_prompt.reshape(batch * n_mem, D_MODEL)

    outs = {name: [] for name in ("kp", "vp", "ks", "vs", "mk", "mv", "pp", "ps", "cp", "cs", "hp", "hs")}
    for l in range(depth):
        lam_init = 0.8 - 0.6 * math.exp(-0.3 * l)
        lam_vecs = jnp.stack([lam_q1[l], lam_k1[l], lam_q2[l], lam_k2[l]])
        subg = attn_sub_g[l].reshape(1, HEAD_W)
        lru_w = _lru_weights(conv_w[l], conv_b[l], lru_wa[l], lru_ba[l], lru_wx[l], lru_bx[l],
                             lru_lambda[l])

        mn = _rmsnorm(mem_rows, mem_g[l], EPS, BF16)
        mk = _matmul(mn, w_mem_k[l])
        mv = _matmul(mn, w_mem_v[l])
        proj = _layer_dense_in(xp, ln_g[l], w_in[l])
        k_f, v_f, k_bf, v_bf, q_st = _rope_call(proj, cos_p, sin_p, seq_len=seq_len, tm=rope_tm)
        h_a = _attn_prompt(q_st, k_bf, v_bf, proj, subg, lam_vecs, lam_init,
                           batch=batch, seq_len=seq_len)
        h_b = _pool_prompt(proj, pool_w[l], pool_scale[l], seq_len=seq_len)
        h_c, h_last = _lru_prompt(proj, lru_w, batch=batch, seq_len=seq_len)
        h_d = _mem_attn(proj, mk, mv, n_seq=batch, rows_per_seq=seq_len, tq=512)
        proj3 = proj.reshape(batch, seq_len, -1)
        outs["pp"].append(proj3[:, seq_len - POOL_BUF:, COL_U * BRANCH_W:(COL_U + 1) * BRANCH_W])
        outs["cp"].append(proj3[:, seq_len - (CONV_W - 1):, COL_X * BRANCH_W:(COL_X + 1) * BRANCH_W])
        xp = _layer_dense_out(xp, h_a, h_b, h_c, h_d, proj, w_branch[l], w_out[l])
        outs["kp"].append(k_f.reshape(batch, seq_len, 2 * A_HEADS, A_HD))
        outs["vp"].append(v_f.reshape(batch, seq_len, A_HEADS, HEAD_W))
        outs["mk"].append(mk.reshape(batch, n_mem, M_HEADS, M_HD))
        outs["mv"].append(mv.reshape(batch, n_mem, M_HEADS, M_HD))
        outs["hp"].append(h_last.reshape(batch, BRANCH_W))

        proj = _layer_dense_in(xs, ln_g[l], w_in[l])
        k_f, v_f, k_bf, v_bf, q_bf = _rope_call(proj, cos_s, sin_s, seq_len=0, tm=rope_tm)
        h_a = _attn_sample(q_bf, k_bf, v_bf, cache_k, cache_v, l, page_table, proj, subg,
                           lam_vecs, lam_init, n_seq=n_seq, t=t)
        pool_buf = jnp.pad(state_pool[l], ((0, 0), (POOL_PAD - POOL_BUF, 0), (0, 0)))
        h_b = _pool_sample(proj, pool_buf, pool_w[l], pool_scale[l], t=t, past_len=past_len)
        conv_buf = jnp.pad(state_conv[l], ((0, 0), (CONV_PAD - (CONV_W - 1), 0), (0, 0)))
        h0_rows = jnp.repeat(state_h[l], t, axis=0)
        h_c, hs_rows = _lru_sample(proj, conv_buf, h0_rows, lru_w, t=t)
        h_d = _mem_attn(proj, cache_mem_k[l].reshape(n_seq * n_mem, BRANCH_W),
                        cache_mem_v[l].reshape(n_seq * n_mem, BRANCH_W),
                        n_seq=n_seq, rows_per_seq=t, tq=t)
        proj3 = proj.reshape(n_seq, t, -1)
        u_s = proj3[:, :, COL_U * BRANCH_W:(COL_U + 1) * BRANCH_W]
        x_s = proj3[:, :, COL_X * BRANCH_W:(COL_X + 1) * BRANCH_W]
        outs["ps"].append(jnp.concatenate([state_pool[l], u_s], axis=1)[:, -POOL_BUF:])
        outs["cs"].append(jnp.concatenate([state_conv[l], x_s], axis=1)[:, -(CONV_W - 1):])
        xs = _layer_dense_out(xs, h_a, h_b, h_c, h_d, proj, w_branch[l], w_out[l])
        outs["ks"].append(k_f.reshape(n_seq, t, 2 * A_HEADS, A_HD))
        outs["vs"].append(v_f.reshape(n_seq, t, A_HEADS, HEAD_W))
        outs["hs"].append(hs_rows.reshape(n_seq, t, BRANCH_W)[:, t - 1])

    y_prompt = _rmsnorm(xp, final_g, EPS, F32).reshape(batch, seq_len, D_MODEL)
    y_sample = _rmsnorm(xs, final_g, EPS, F32).reshape(n_seq, t, D_MODEL)
    st = lambda name: jnp.stack(outs[name])
    return (y_prompt, y_sample, st("kp"), st("vp"), st("ks"), st("vs"), st("mk"), st("mv"),
            st("pp"), st("ps"), st("cp"), st("cs"), st("hp"), st("hs"))
```

```python
import functools
import math

import jax
import jax.numpy as jnp
from jax import lax
from jax.experimental import pallas as pl
from jax.experimental.pallas import tpu as pltpu

F32 = jnp.float32
BF16 = jnp.bfloat16

D_MODEL = 2048
BRANCH_W = 1024
A_HEADS = 8
A_HD = 64
HEAD_W = 2 * A_HD
ROPE_THETA = 10000.0
SUBLN_EPS = 1e-5
POOL_WINDOWS = (2, 4, 8, 16)
POOL_GW = BRANCH_W // len(POOL_WINDOWS)
POOL_BUF = max(POOL_WINDOWS) - 1
POOL_PAD = 16
LRU_BLOCKS = 8
LRU_BW = BRANCH_W // LRU_BLOCKS
CONV_W = 4
CONV_PAD = 8
LRU_C = 8.0
M_HEADS = 4
M_HD = BRANCH_W // M_HEADS
N_BRANCH = 4
PAGE_SIZE = 128
EPS = 1e-6
NEG = -1e30
SUBLANES = 8

COL_Q, COL_K, COL_V, COL_U, COL_X, COL_QD = 0, 1, 2, 3, 4, 5
COL_Z = 6
COL_G = 10

VMEM_LIMIT = 56 * 1024 * 1024


def _params(*sem):
    return pltpu.CompilerParams(dimension_semantics=sem, vmem_limit_bytes=VMEM_LIMIT)


def _silu(z):
    return z * jax.nn.sigmoid(z)


def _rmsnorm_kernel(x_ref, g_ref, o_ref, *, eps):
    x = x_ref[...]
    y = x * lax.rsqrt(jnp.mean(x * x, axis=-1, keepdims=True) + eps)
    o_ref[...] = (y * g_ref[...]).astype(o_ref.dtype)


def _rmsnorm(x, g, eps, out_dtype, tm=256):
    m, d = x.shape
    return pl.pallas_call(
        functools.partial(_rmsnorm_kernel, eps=eps),
        out_shape=jax.ShapeDtypeStruct((m, d), out_dtype),
        grid=(m // tm,),
        in_specs=[pl.BlockSpec((tm, d), lambda i: (i, 0)),
                  pl.BlockSpec((1, d), lambda i: (0, 0))],
        out_specs=pl.BlockSpec((tm, d), lambda i: (i, 0)),
        compiler_params=_params("parallel"),
        name="rmsnorm",
    )(x, g.reshape(1, d))


def _mm_kernel(a_ref, w_ref, *rest, has_res):
    if has_res:
        r_ref, o_ref, wb_ref = rest
    else:
        o_ref, wb_ref = rest

    @pl.when(pl.program_id(1) == 0)
    def _():
        wb_ref[...] = w_ref[...].astype(BF16)

    acc = jnp.dot(a_ref[...], wb_ref[...], preferred_element_type=F32)
    if has_res:
        acc = r_ref[...] + acc
    o_ref[...] = acc


def _matmul(a, w, layer, res=None, tm=1024, tn=1024):
    m, k = a.shape
    n = w.shape[2]
    tm = min(tm, m)
    in_specs = [pl.BlockSpec((tm, k), lambda j, i: (i, 0)),
                pl.BlockSpec((None, k, tn), lambda j, i: (layer, 0, j))]
    args = [a, w]
    if res is not None:
        in_specs.append(pl.BlockSpec((tm, tn), lambda j, i: (i, j)))
        args.append(res)
    return pl.pallas_call(
        functools.partial(_mm_kernel, has_res=res is not None),
        out_shape=jax.ShapeDtypeStruct((m, n), F32),
        grid=(n // tn, m // tm),
        in_specs=in_specs,
        out_specs=pl.BlockSpec((tm, tn), lambda j, i: (i, j)),
        scratch_shapes=[pltpu.VMEM((k, tn), BF16)],
        compiler_params=_params("arbitrary", "arbitrary"),
        name="matmul",
    )(*args)


def _rope(x, cos, sin_signed):
    lane = lax.broadcasted_iota(jnp.int32, x.shape, 1)
    first_half = (lane % A_HD) < (A_HD // 2)
    width = x.shape[1]
    partner = jnp.where(first_half,
                        pltpu.roll(x, width - A_HD // 2, 1),
                        pltpu.roll(x, A_HD // 2, 1))
    reps = width // HEAD_W
    return x * jnp.tile(cos, (1, reps)) + partner * jnp.tile(sin_signed, (1, reps))


def _rope_kernel(q_ref, k_ref, v_ref, cos_ref, sin_ref,
                 ko_ref, vo_ref, kb_ref, vb_ref, qo_ref, *, stacked_q):
    cos = cos_ref[...]
    sin = sin_ref[...]
    k = _rope(k_ref[...], cos, sin)
    ko_ref[...] = k
    kb_ref[...] = k.astype(BF16)
    v = v_ref[...]
    vo_ref[...] = v
    vb_ref[...] = v.astype(BF16)
    q = _rope(q_ref[...], cos, sin) * (A_HD ** -0.5)
    if stacked_q:
        tm = q.shape[0]
        lane = lax.broadcasted_iota(jnp.int32, (tm, HEAD_W), 1)
        for h in range(A_HEADS):
            qh = q[:, h * HEAD_W:(h + 1) * HEAD_W]
            qo_ref[h, 0] = jnp.where(lane < A_HD, qh, 0.0).astype(BF16)
            qo_ref[h, 1] = jnp.where(lane >= A_HD, qh, 0.0).astype(BF16)
    else:
        qo_ref[...] = q.astype(BF16)


def _rope_call(proj, cos, sin_signed, *, seq_len, tm):
    m = proj.shape[0]
    stacked = seq_len > 0
    if stacked:
        tps = seq_len // tm
        tab_map = lambda i: (i % tps, 0)
        q_shape = jax.ShapeDtypeStruct((m // seq_len, A_HEADS, 2, seq_len, HEAD_W), BF16)
        q_spec = pl.BlockSpec((None, A_HEADS, 2, tm, HEAD_W),
                              lambda i: (i // tps, 0, 0, i % tps, 0))
    else:
        tab_map = lambda i: (0, 0)
        q_shape = jax.ShapeDtypeStruct((m, BRANCH_W), BF16)
        q_spec = pl.BlockSpec((tm, BRANCH_W), lambda i: (i, 0))
    col = lambda c: pl.BlockSpec((tm, BRANCH_W), lambda i, c=c: (i, c))
    row = pl.BlockSpec((tm, BRANCH_W), lambda i: (i, 0))
    return pl.pallas_call(
        functools.partial(_rope_kernel, stacked_q=stacked),
        out_shape=(jax.ShapeDtypeStruct((m, BRANCH_W), F32),
                   jax.ShapeDtypeStruct((m, BRANCH_W), F32),
                   jax.ShapeDtypeStruct((m, BRANCH_W), BF16),
                   jax.ShapeDtypeStruct((m, BRANCH_W), BF16),
                   q_shape),
        grid=(m // tm,),
        in_specs=[col(COL_Q), col(COL_K), col(COL_V),
                  pl.BlockSpec((tm, HEAD_W), tab_map),
                  pl.BlockSpec((tm, HEAD_W), tab_map)],
        out_specs=(row, row, row, row, q_spec),
        compiler_params=_params("parallel"),
        name="rope",
    )(proj, proj, proj, cos, sin_signed)


def _rope_tables(pos):
    half = A_HD // 2
    inv = 1.0 / (ROPE_THETA ** (jnp.arange(half, dtype=F32) / half))
    ang = pos.astype(F32)[:, None] * inv[None, :]
    cos, sin = jnp.cos(ang), jnp.sin(ang)
    cos = jnp.concatenate([cos, cos], axis=-1)
    sin = jnp.concatenate([-sin, sin], axis=-1)
    return jnp.tile(cos, (1, 2)), jnp.tile(sin, (1, 2))


def _lambda(lam_ref, lam_init):
    lv = lam_ref[...]
    l1 = jnp.sum(lv[0:1] * lv[1:2], axis=-1, keepdims=True)
    l2 = jnp.sum(lv[2:3] * lv[3:4], axis=-1, keepdims=True)
    return jnp.exp(l1) - jnp.exp(l2) + lam_init


def _diff_head_out(o1, o2, lam, subg, z, lam_init):
    d = o1 - lam * o2
    y = d * lax.rsqrt(jnp.mean(d * d, axis=-1, keepdims=True) + SUBLN_EPS) * subg
    return (y * (1.0 - lam_init)) * _silu(z)


_NT = (((1,), (1,)), ((), ()))
LANES = 128


def _lane_block_sum(p):
    out = p[:, :LANES]
    for c in range(1, p.shape[1] // LANES):
        out = out + p[:, c * LANES:(c + 1) * LANES]
    return out


def _softmax_block(s, m_prev):
    m_new = jnp.maximum(m_prev, jnp.max(s, axis=-1, keepdims=True))
    alpha = jnp.exp(m_prev - m_new)
    p = jnp.exp(s - jnp.tile(m_new, (1, s.shape[1] // LANES)))
    return m_new, alpha, p


def _attn_p_kernel(q_ref, k_ref, v_ref, z_ref, subg_ref, lam_ref, o_ref,
                   m_ref, l_ref, acc_ref, *, tq, tk, hpb, lam_init):
    qi = pl.program_id(2)
    rows = 2 * tq
    m_ref[...] = jnp.full(m_ref.shape, NEG, F32)
    l_ref[...] = jnp.zeros(l_ref.shape, F32)
    acc_ref[...] = jnp.zeros(acc_ref.shape, F32)

    def block(start, masked):
        for j in range(hpb):
            cols = slice(j * HEAD_W, (j + 1) * HEAD_W)
            q = q_ref[j].reshape(rows, HEAD_W)
            s = lax.dot_general(q, k_ref[pl.ds(start, tk), cols], _NT,
                                preferred_element_type=F32)
            if masked:
                r = lax.broadcasted_iota(jnp.int32, s.shape, 0)
                c = lax.broadcasted_iota(jnp.int32, s.shape, 1)
                s = jnp.where(start + c <= qi * tq + r % tq, s, NEG)
            m_new, alpha, p = _softmax_block(s, m_ref[j])
            l_ref[j] = alpha * l_ref[j] + _lane_block_sum(p)
            acc_ref[j] = alpha * acc_ref[j] + jnp.dot(
                p.astype(BF16), v_ref[pl.ds(start, tk), cols], preferred_element_type=F32)
            m_ref[j] = m_new

    n_full = (qi * tq) // tk

    def body(ki, carry):
        block(pl.multiple_of(ki * tk, tk), False)
        return carry

    lax.fori_loop(0, n_full, body, 0)
    block(pl.multiple_of(n_full * tk, tk), True)

    lam = _lambda(lam_ref, lam_init)
    for j in range(hpb):
        cols = slice(j * HEAD_W, (j + 1) * HEAD_W)
        o = acc_ref[j] / jnp.sum(l_ref[j], axis=-1, keepdims=True)
        out = _diff_head_out(o[:tq], o[tq:], lam, subg_ref[...], z_ref[:, cols], lam_init)
        o_ref[:, cols] = out.astype(o_ref.dtype)


def _attn_prompt(q_st, k_bf, v_bf, proj, subg, lam_vecs, lam_init, *, batch, seq_len,
                 tq=256, tk=512, hpb=2):
    nq = seq_len // tq
    gw = hpb * HEAD_W
    zcol = COL_Z * (BRANCH_W // gw)
    kv_spec = pl.BlockSpec((seq_len, gw), lambda b, g, qi: (b, g))
    return pl.pallas_call(
        functools.partial(_attn_p_kernel, tq=tq, tk=tk, hpb=hpb, lam_init=lam_init),
        out_shape=jax.ShapeDtypeStruct((batch * seq_len, BRANCH_W), BF16),
        grid=(batch, A_HEADS // hpb, nq),
        in_specs=[pl.BlockSpec((None, hpb, 2, tq, HEAD_W), lambda b, g, qi: (b, g, 0, qi, 0)),
                  kv_spec, kv_spec,
                  pl.BlockSpec((tq, gw), lambda b, g, qi: (b * nq + qi, zcol + g)),
                  pl.BlockSpec((1, HEAD_W), lambda b, g, qi: (0, 0)),
                  pl.BlockSpec((4, A_HD), lambda b, g, qi: (0, 0))],
        out_specs=pl.BlockSpec((tq, gw), lambda b, g, qi: (b * nq + qi, g)),
        scratch_shapes=[pltpu.VMEM((hpb, 2 * tq, LANES), F32),
                        pltpu.VMEM((hpb, 2 * tq, LANES), F32),
                        pltpu.VMEM((hpb, 2 * tq, HEAD_W), F32)],
        compiler_params=_params("parallel", "parallel", "arbitrary"),
        name="attn_prompt",
    )(q_st, k_bf, v_bf, proj, subg, lam_vecs)


def _attn_s_kernel(pt_ref, q_ref, kn_ref, vn_ref, *rest, t, pps, lam_init):
    del pt_ref
    kc_refs, vc_refs = rest[:pps], rest[pps:2 * pps]
    (z_ref, subg_ref, lam_ref, o_ref,
     qbd_ref, knp_ref, vnp_ref, exp_ref, hm_ref, m_ref, l_ref, acc_ref) = rest[2 * pps:]
    p_idx = pl.program_id(1)
    hrows = 2 * t

    def stats(s):
        m_new, alpha, p = _softmax_block(s, m_ref[...])
        l_ref[...] = alpha * l_ref[...] + _lane_block_sum(p)
        m_ref[...] = m_new
        return alpha, p.astype(BF16)

    @pl.when(p_idx == 0)
    def _():
        q = q_ref[...]
        lane = lax.broadcasted_iota(jnp.int32, q.shape, 1)
        for hc in range(2 * A_HEADS):
            keep = (lane >= hc * A_HD) & (lane < (hc + 1) * A_HD)
            qbd_ref[hc * t:(hc + 1) * t, :] = jnp.where(keep, q, jnp.zeros_like(q))
        er = lax.broadcasted_iota(jnp.int32, exp_ref.shape, 0)
        ec = lax.broadcasted_iota(jnp.int32, exp_ref.shape, 1)
        exp_ref[...] = jnp.where(ec // A_HEADS == er, 1.0, 0.0).astype(BF16)
        hr = lax.broadcasted_iota(jnp.int32, hm_ref.shape, 0)
        hc_ = lax.broadcasted_iota(jnp.int32, hm_ref.shape, 1)
        hm_ref[...] = jnp.where(hc_ % A_HEADS == hr // hrows, 1.0, 0.0).astype(BF16)
        m_ref[...] = jnp.full(m_ref.shape, NEG, F32)
        l_ref[...] = jnp.zeros(l_ref.shape, F32)
        knp_ref[...] = jnp.zeros(knp_ref.shape, BF16)
        vnp_ref[...] = jnp.zeros(vnp_ref.shape, BF16)
        knp_ref[0:t, :] = kn_ref[...]
        vnp_ref[0:t, :] = vn_ref[...]
        s = lax.dot_general(qbd_ref[...], knp_ref[...], _NT, preferred_element_type=F32)
        r = lax.broadcasted_iota(jnp.int32, s.shape, 0)
        c = lax.broadcasted_iota(jnp.int32, s.shape, 1)
        s = jnp.where(c <= (r % t), s, NEG)
        _, pb = stats(s)
        acc_ref[...] = jnp.concatenate(
            [jnp.dot(pb[h * hrows:(h + 1) * hrows], vnp_ref[:, h * HEAD_W:(h + 1) * HEAD_W],
                     preferred_element_type=F32) for h in range(A_HEADS)], axis=0)

    kcat = jnp.concatenate(
        [r[...].reshape(BRANCH_W, PAGE_SIZE).astype(BF16) for r in kc_refs], axis=1)
    s = jnp.dot(qbd_ref[...], kcat, preferred_element_type=F32)
    alpha, pb = stats(s)
    pv = None
    for i, vc_ref in enumerate(vc_refs):
        spread = jnp.dot(pb[:, i * PAGE_SIZE:(i + 1) * PAGE_SIZE], exp_ref[...],
                         preferred_element_type=F32).astype(BF16)
        vc = vc_ref[...].reshape(PAGE_SIZE * A_HEADS, HEAD_W).astype(BF16)
        part = jnp.dot(spread * hm_ref[...], vc, preferred_element_type=F32)
        pv = part if pv is None else pv + part
    acc_ref[...] = alpha * acc_ref[...] + pv

    @pl.when(p_idx == pl.num_programs(1) - 1)
    def _():
        o = acc_ref[...] / jnp.sum(l_ref[...], axis=-1, keepdims=True)
        lam = _lambda(lam_ref, lam_init)
        subg = subg_ref[...]
        for h in range(A_HEADS):
            o1 = o[(2 * h) * t:(2 * h + 1) * t]
            o2 = o[(2 * h + 1) * t:(2 * h + 2) * t]
            zh = z_ref[:, h * HEAD_W:(h + 1) * HEAD_W]
            out = _diff_head_out(o1, o2, lam, subg, zh, lam_init)
            o_ref[:, h * HEAD_W:(h + 1) * HEAD_W] = out.astype(o_ref.dtype)


def _attn_sample(q_bf, k_bf, v_bf, cache_k, cache_v, layer, page_table, proj, subg, lam_vecs,
                 lam_init, *, n_seq, t, pps=4):
    n_pages = page_table.shape[1]
    rows = 2 * A_HEADS * t
    pt_flat = page_table.reshape(-1)
    kc = jnp.transpose(cache_k, (0, 1, 3, 4, 2))
    seq_spec = pl.BlockSpec((t, BRANCH_W), lambda b, p, pt: (b, 0))

    def page(i):
        return lambda b, p, pt: (layer, pt[b * n_pages + p * pps + i], 0, 0, 0)

    k_specs = [pl.BlockSpec((None, None, 2 * A_HEADS, A_HD, PAGE_SIZE), page(i))
               for i in range(pps)]
    v_specs = [pl.BlockSpec((None, None, PAGE_SIZE, A_HEADS, HEAD_W), page(i))
               for i in range(pps)]
    grid_spec = pltpu.PrefetchScalarGridSpec(
        num_scalar_prefetch=1,
        grid=(n_seq, n_pages // pps),
        in_specs=[seq_spec, seq_spec, seq_spec] + k_specs + v_specs + [
            pl.BlockSpec((t, BRANCH_W), lambda b, p, pt: (b, COL_Z)),
            pl.BlockSpec((1, HEAD_W), lambda b, p, pt: (0, 0)),
            pl.BlockSpec((4, A_HD), lambda b, p, pt: (0, 0))],
        out_specs=pl.BlockSpec((t, BRANCH_W), lambda b, p, pt: (b, 0)),
        scratch_shapes=[pltpu.VMEM((rows, BRANCH_W), BF16),
                        pltpu.VMEM((PAGE_SIZE, BRANCH_W), BF16),
                        pltpu.VMEM((PAGE_SIZE, BRANCH_W), BF16),
                        pltpu.VMEM((PAGE_SIZE, PAGE_SIZE * A_HEADS), BF16),
                        pltpu.VMEM((rows, PAGE_SIZE * A_HEADS), BF16),
                        pltpu.VMEM((rows, LANES), F32), pltpu.VMEM((rows, LANES), F32),
                        pltpu.VMEM((rows, HEAD_W), F32)],
    )
    return pl.pallas_call(
        functools.partial(_attn_s_kernel, t=t, pps=pps, lam_init=lam_init),
        out_shape=jax.ShapeDtypeStruct((n_seq * t, BRANCH_W), BF16),
        grid_spec=grid_spec,
        compiler_params=_params("parallel", "arbitrary"),
        name="attn_sample",
    )(pt_flat, q_bf, k_bf, v_bf, *([kc] * pps), *([cache_v] * pps), proj, subg, lam_vecs)


def _window_sum(e, log2_win):
    s = e
    for k in range(log2_win):
        s = s + pltpu.roll(s, 1 << k, 0)
    return s


def _pool_groups(ext, pos, w_ref, scale_ref, z, o_ref, pick):
    for gi, win in enumerate(POOL_WINDOWS):
        c0, c1 = gi * POOL_GW, (gi + 1) * POOL_GW
        e = ext[:, c0:c1]
        cur = pick(e)
        tot = pick(_window_sum(e, gi + 1))
        cnt = jnp.minimum(pos + 1, win).astype(F32)
        mixed = tot / cnt - cur
        out = jnp.dot(mixed.astype(BF16), w_ref[gi].astype(BF16), preferred_element_type=F32)
        out = out * scale_ref[:, c0:c1]
        o_ref[:, c0:c1] = (out * _silu(z[:, c0:c1])).astype(o_ref.dtype)


def _pool_p_kernel(u_ref, prev_ref, z_ref, w_ref, scale_ref, o_ref, *, tm, tps):
    i = pl.program_id(0)
    first = (i % tps) == 0
    prev = jnp.where(first, 0.0, prev_ref[...])
    ext = jnp.concatenate([prev, u_ref[...]], axis=0)
    pos = (i % tps) * tm + lax.broadcasted_iota(jnp.int32, (tm, POOL_GW), 0)
    _pool_groups(ext, pos, w_ref, scale_ref, z_ref[...], o_ref, lambda x: x[POOL_PAD:])


def _pool_s_kernel(u_ref, buf_ref, z_ref, w_ref, scale_ref, o_ref, *, nb, t, past_len):
    u3 = u_ref[...].reshape(nb, t, BRANCH_W)
    ext = jnp.concatenate([buf_ref[...], u3], axis=1)
    ext = ext.reshape(nb * (POOL_PAD + t), BRANCH_W)
    pos = past_len + lax.broadcasted_iota(jnp.int32, (nb * t, POOL_GW), 0) % t

    def pick(x):
        x3 = x.reshape(nb, POOL_PAD + t, x.shape[-1])
        return x3[:, POOL_PAD:].reshape(nb * t, x.shape[-1])

    _pool_groups(ext, pos, w_ref, scale_ref, z_ref[...], o_ref, pick)


def _pool_prompt(proj, pool_w, pool_scale, layer, *, seq_len, tm=256):
    m = proj.shape[0]
    tps = seq_len // tm
    per = tm // POOL_PAD
    return pl.pallas_call(
        functools.partial(_pool_p_kernel, tm=tm, tps=tps),
        out_shape=jax.ShapeDtypeStruct((m, BRANCH_W), BF16),
        grid=(m // tm,),
        in_specs=[pl.BlockSpec((tm, BRANCH_W), lambda i: (i, COL_U)),
                  pl.BlockSpec((POOL_PAD, BRANCH_W),
                               lambda i: (jnp.maximum(i * per - 1, 0), COL_U)),
                  pl.BlockSpec((tm, BRANCH_W), lambda i: (i, COL_Z + 1)),
                  pl.BlockSpec((None,) + pool_w.shape[1:], lambda i: (layer, 0, 0, 0)),
                  pl.BlockSpec((1, BRANCH_W), lambda i: (0, 0))],
        out_specs=pl.BlockSpec((tm, BRANCH_W), lambda i: (i, 0)),
        compiler_params=_params("parallel"),
        name="pool_prompt",
    )(proj, proj, proj, pool_w, pool_scale.reshape(1, BRANCH_W))


def _pool_sample(proj, buf_padded, pool_w, pool_scale, layer, *, t, past_len, nb=32):
    m = proj.shape[0]
    return pl.pallas_call(
        functools.partial(_pool_s_kernel, nb=nb, t=t, past_len=past_len),
        out_shape=jax.ShapeDtypeStruct((m, BRANCH_W), BF16),
        grid=(m // (nb * t),),
        in_specs=[pl.BlockSpec((nb * t, BRANCH_W), lambda i: (i, COL_U)),
                  pl.BlockSpec((nb, POOL_PAD, BRANCH_W), lambda i: (i, 0, 0)),
                  pl.BlockSpec((nb * t, BRANCH_W), lambda i: (i, COL_Z + 1)),
                  pl.BlockSpec((None,) + pool_w.shape[1:], lambda i: (layer, 0, 0, 0)),
                  pl.BlockSpec((1, BRANCH_W), lambda i: (0, 0))],
        out_specs=pl.BlockSpec((nb * t, BRANCH_W), lambda i: (i, 0)),
        compiler_params=_params("parallel"),
        name="pool_sample",
    )(proj, buf_padded, proj, pool_w, pool_scale.reshape(1, BRANCH_W))


def _conv_taps(ext, pick, cw_ref, cb_ref):
    out = cb_ref[...] + pick(ext) * cw_ref[CONV_W - 1:CONV_W, :]
    for k in range(CONV_W - 1):
        shift = CONV_W - 1 - k
        out = out + pick(pltpu.roll(ext, shift, 0)) * cw_ref[k:k + 1, :]
    return out


def _block_diag_gate(xb, w_ref, b_ref):
    parts = [jnp.dot(xb[:, n * LRU_BW:(n + 1) * LRU_BW], w_ref[n].astype(BF16),
                     preferred_element_type=F32) for n in range(LRU_BLOCKS)]
    return jax.nn.sigmoid(jnp.concatenate(parts, axis=-1) + b_ref[...])


def _lru_coeffs(xc, wa_ref, ba_ref, wx_ref, bx_ref, lam_ref):
    xb = xc.astype(BF16)
    r = _block_diag_gate(xb, wa_ref, ba_ref)
    i = _block_diag_gate(xb, wx_ref, bx_ref)
    nl = -lam_ref[...]
    softplus = jnp.maximum(nl, 0.0) + jnp.log1p(jnp.exp(-jnp.abs(nl)))
    log_a = -LRU_C * r * softplus
    a = jnp.exp(log_a)
    b = jnp.sqrt(1.0 - jnp.exp(2.0 * log_a)) * (i * xc)
    return a, b


def _group_scan(a, b):
    row = lax.broadcasted_iota(jnp.int32, a.shape, 0) % SUBLANES
    d = 1
    while d < SUBLANES:
        valid = row >= d
        a_sh = pltpu.roll(a, d, 0)
        b_sh = pltpu.roll(b, d, 0)
        b = jnp.where(valid, a * b_sh + b, b)
        a = jnp.where(valid, a * a_sh, a)
        d *= 2
    return a, b


def _lru_p_kernel(x_ref, prev_ref, z_ref, cw_ref, cb_ref, wa_ref, ba_ref, wx_ref, bx_ref,
                  lam_ref, o_ref, hl_ref, a_s, b_s, h_s, *, tm, tps):
    i = pl.program_id(0)
    first = (i % tps) == 0

    @pl.when(first)
    def _():
        h_s[...] = jnp.zeros(h_s.shape, F32)

    prev = jnp.where(first, 0.0, prev_ref[...])
    ext = jnp.concatenate([prev, x_ref[...]], axis=0)
    xc = _conv_taps(ext, lambda x: x[CONV_PAD:], cw_ref, cb_ref)
    a, b = _lru_coeffs(xc, wa_ref, ba_ref, wx_ref, bx_ref, lam_ref)
    a, b = _group_scan(a, b)
    a_s[...] = a
    b_s[...] = b

    def body(g, h):
        r0 = pl.multiple_of(g * SUBLANES, SUBLANES)
        hs = a_s[pl.ds(r0, SUBLANES), :] * h + b_s[pl.ds(r0, SUBLANES), :]
        b_s[pl.ds(r0, SUBLANES), :] = hs
        return hs[SUBLANES - 1:SUBLANES, :]

    h = lax.fori_loop(0, tm // SUBLANES, body, h_s[...])
    h_s[...] = h
    hl_ref[...] = h
    o_ref[...] = (b_s[...] * _silu(z_ref[...])).astype(o_ref.dtype)


def _lru_s_kernel(x_ref, buf_ref, h0_ref, z_ref, cw_ref, cb_ref, wa_ref, ba_ref, wx_ref, bx_ref,
                  lam_ref, o_ref, hs_ref, *, nb, t):
    x3 = x_ref[...].reshape(nb, t, BRANCH_W)
    ext = jnp.concatenate([buf_ref[...], x3], axis=1).reshape(nb * (CONV_PAD + t), BRANCH_W)

    def pick(x):
        return x.reshape(nb, CONV_PAD + t, BRANCH_W)[:, CONV_PAD:].reshape(nb * t, BRANCH_W)

    xc = _conv_taps(ext, pick, cw_ref, cb_ref)
    a, b = _lru_coeffs(xc, wa_ref, ba_ref, wx_ref, bx_ref, lam_ref)
    a, b = _group_scan(a, b)
    hs = a * h0_ref[...] + b
    hs_ref[...] = hs
    o_ref[...] = (hs * _silu(z_ref[...])).astype(o_ref.dtype)


def _lru_weight_specs(nargs_map):
    full = lambda shape: pl.BlockSpec(shape, lambda *a: (0,) * len(shape))
    del nargs_map
    return [full((CONV_W, BRANCH_W)), full((1, BRANCH_W)),
            full((LRU_BLOCKS, LRU_BW, LRU_BW)), full((1, BRANCH_W)),
            full((LRU_BLOCKS, LRU_BW, LRU_BW)), full((1, BRANCH_W)),
            full((1, BRANCH_W))]


def _lru_weights(conv_w, conv_b, wa, ba, wx, bx, lam):
    r = lambda v: v.reshape(1, BRANCH_W)
    return (conv_w, r(conv_b), wa, r(ba), wx, r(bx), r(lam))


def _lru_prompt(proj, weights, *, batch, seq_len, tm=256):
    m = proj.shape[0]
    tps = seq_len // tm
    per = tm // CONV_PAD
    return pl.pallas_call(
        functools.partial(_lru_p_kernel, tm=tm, tps=tps),
        out_shape=(jax.ShapeDtypeStruct((m, BRANCH_W), BF16),
                   jax.ShapeDtypeStruct((batch, 1, BRANCH_W), F32)),
        grid=(m // tm,),
        in_specs=[pl.BlockSpec((tm, BRANCH_W), lambda i: (i, COL_X)),
                  pl.BlockSpec((CONV_PAD, BRANCH_W),
                               lambda i: (jnp.maximum(i * per - 1, 0), COL_X)),
                  pl.BlockSpec((tm, BRANCH_W), lambda i: (i, COL_Z + 2))]
        + _lru_weight_specs(None),
        out_specs=(pl.BlockSpec((tm, BRANCH_W), lambda i: (i, 0)),
                   pl.BlockSpec((None, 1, BRANCH_W), lambda i: (i // tps, 0, 0))),
        scratch_shapes=[pltpu.VMEM((tm, BRANCH_W), F32), pltpu.VMEM((tm, BRANCH_W), F32),
                        pltpu.VMEM((1, BRANCH_W), F32)],
        compiler_params=_params("arbitrary"),
        name="lru_prompt",
    )(proj, proj, proj, *weights)


def _lru_sample(proj, buf_padded, h0_rows, weights, *, t, nb=32):
    m = proj.shape[0]
    rows = nb * t
    return pl.pallas_call(
        functools.partial(_lru_s_kernel, nb=nb, t=t),
        out_shape=(jax.ShapeDtypeStruct((m, BRANCH_W), BF16),
                   jax.ShapeDtypeStruct((m, BRANCH_W), F32)),
        grid=(m // rows,),
        in_specs=[pl.BlockSpec((rows, BRANCH_W), lambda i: (i, COL_X)),
                  pl.BlockSpec((nb, CONV_PAD, BRANCH_W), lambda i: (i, 0, 0)),
                  pl.BlockSpec((rows, BRANCH_W), lambda i: (i, 0)),
                  pl.BlockSpec((rows, BRANCH_W), lambda i: (i, COL_Z + 2))]
        + _lru_weight_specs(None),
        out_specs=(pl.BlockSpec((rows, BRANCH_W), lambda i: (i, 0)),
                   pl.BlockSpec((rows, BRANCH_W), lambda i: (i, 0))),
        compiler_params=_params("parallel"),
        name="lru_sample",
    )(proj, buf_padded, h0_rows, proj, *weights)


def _mem_attn_kernel(q_ref, k_ref, v_ref, z_ref, o_ref):
    q = q_ref[...].astype(BF16)
    k = k_ref[...].astype(BF16)
    v = v_ref[...].astype(BF16)
    z = z_ref[...]
    for h in range(M_HEADS):
        c0, c1 = h * M_HD, (h + 1) * M_HD
        s = lax.dot_general(q[:, c0:c1], k[:, c0:c1], _NT, preferred_element_type=F32)
        s = s * (M_HD ** -0.5)
        p = jnp.exp(s - jnp.max(s, axis=-1, keepdims=True))
        den = jnp.sum(p, axis=-1, keepdims=True)
        o = jnp.dot((p / den).astype(BF16), v[:, c0:c1], preferred_element_type=F32)
        o_ref[:, c0:c1] = (o * _silu(z[:, c0:c1])).astype(o_ref.dtype)


def _mem_attn(proj, mem_k, mem_v, layer, *, n_seq, rows_per_seq, tq):
    n_mem = mem_k.shape[1] // n_seq
    nq = rows_per_seq // tq
    kv_spec = pl.BlockSpec((None, n_mem, BRANCH_W), lambda b, qi: (layer, b, 0))
    return pl.pallas_call(
        _mem_attn_kernel,
        out_shape=jax.ShapeDtypeStruct((n_seq * rows_per_seq, BRANCH_W), BF16),
        grid=(n_seq, nq),
        in_specs=[pl.BlockSpec((tq, BRANCH_W), lambda b, qi: (b * nq + qi, COL_QD)),
                  kv_spec, kv_spec,
                  pl.BlockSpec((tq, BRANCH_W), lambda b, qi: (b * nq + qi, COL_Z + 3))],
        out_specs=pl.BlockSpec((tq, BRANCH_W), lambda b, qi: (b * nq + qi, 0)),
        compiler_params=_params("parallel", "parallel"),
        name="mem_attn",
    )(proj, mem_k, mem_v, proj)


def _merge_kernel(*refs):
    h_refs, g_refs = refs[:N_BRANCH], refs[N_BRANCH:2 * N_BRANCH]
    w_ref, o_ref = refs[2 * N_BRANCH:]
    acc = None
    for n in range(N_BRANCH):
        up = jnp.dot(h_refs[n][...], w_ref[n], preferred_element_type=F32)
        term = jax.nn.sigmoid(g_refs[n][...]) * up
        acc = term if acc is None else acc + term
    o_ref[...] = acc.astype(o_ref.dtype)


def _merge(h_a, h_b, h_c, h_d, proj, w_branch_bf, layer, tm=1024, tn=512):
    m = h_a.shape[0]
    tm = min(tm, m)
    h_spec = pl.BlockSpec((tm, BRANCH_W), lambda i, j: (i, 0))
    gate_col0 = COL_G * BRANCH_W // tn
    g_specs = [pl.BlockSpec((tm, tn), lambda i, j, n=n: (i, gate_col0 + n * (D_MODEL // tn) + j))
               for n in range(N_BRANCH)]
    return pl.pallas_call(
        _merge_kernel,
        out_shape=jax.ShapeDtypeStruct((m, D_MODEL), BF16),
        grid=(m // tm, D_MODEL // tn),
        in_specs=[h_spec] * N_BRANCH + g_specs + [
            pl.BlockSpec((None, N_BRANCH, BRANCH_W, tn), lambda i, j: (layer, 0, 0, j))],
        out_specs=pl.BlockSpec((tm, tn), lambda i, j: (i, j)),
        compiler_params=_params("parallel", "parallel"),
        name="merge",
    )(h_a, h_b, h_c, h_d, proj, proj, proj, proj, w_branch_bf)


def _layer_dense_in(x, ln_g, w_in, layer):
    xn = _rmsnorm(x, ln_g, EPS, BF16)
    return _matmul(xn, w_in, layer)


def _layer_dense_out(x, h_a, h_b, h_c, h_d, proj, w_branch_bf, w_out, layer):
    merged = _merge(h_a, h_b, h_c, h_d, proj, w_branch_bf, layer)
    return _matmul(merged, w_out, layer, res=x, tn=512)


def kernel(x_prompt, x_sample, cache_k, cache_v, cache_mem_k, cache_mem_v, state_pool, state_conv, state_h, page_table, mem_prompt, ln_g, w_in, lam_q1, lam_k1, lam_q2, lam_k2, attn_sub_g, pool_w, pool_scale, conv_w, conv_b, lru_wa, lru_ba, lru_wx, lru_bx, lru_lambda, mem_g, w_mem_k, w_mem_v, w_branch, w_out, final_g):
    batch, seq_len, _ = x_prompt.shape
    n_seq, t, _ = x_sample.shape
    depth = w_in.shape[0]
    n_mem = mem_prompt.shape[1]
    past_len = page_table.shape[1] * PAGE_SIZE
    rope_tm = 256

    cos_p, sin_p = _rope_tables(jnp.arange(seq_len, dtype=jnp.int32))
    cos_s, sin_s = _rope_tables(past_len + jnp.arange(t, dtype=jnp.int32))
    cos_s = jnp.tile(cos_s, (rope_tm // t, 1))
    sin_s = jnp.tile(sin_s, (rope_tm // t, 1))

    xp = x_prompt.reshape(batch * seq_len, D_MODEL)
    xs = x_sample.reshape(n_seq * t, D_MODEL)
    mem_rows = mem_prompt.reshape(batch * n_mem, D_MODEL)

    w_branch_bf = w_branch.astype(BF16)
    mem_k_s = cache_mem_k.reshape(depth, n_seq * n_mem, BRANCH_W)
    mem_v_s = cache_mem_v.reshape(depth, n_seq * n_mem, BRANCH_W)

    outs = {name: [] for name in ("kp", "vp", "ks", "vs", "mk", "mv", "pp", "ps", "cp", "cs", "hp", "hs")}
    for l in range(depth):
        lam_init = 0.8 - 0.6 * math.exp(-0.3 * l)
        lam_vecs = jnp.stack([lam_q1[l], lam_k1[l], lam_q2[l], lam_k2[l]])
        subg = attn_sub_g[l].reshape(1, HEAD_W)
        lru_w = _lru_weights(conv_w[l], conv_b[l], lru_wa[l], lru_ba[l], lru_wx[l], lru_bx[l],
                             lru_lambda[l])

        mn = _rmsnorm(mem_rows, mem_g[l], EPS, BF16)
        mk = _matmul(mn, w_mem_k, l)
        mv = _matmul(mn, w_mem_v, l)
        proj = _layer_dense_in(xp, ln_g[l], w_in, l)
        k_f, v_f, k_bf, v_bf, q_st = _rope_call(proj, cos_p, sin_p, seq_len=seq_len, tm=rope_tm)
        h_a = _attn_prompt(q_st, k_bf, v_bf, proj, subg, lam_vecs, lam_init,
                           batch=batch, seq_len=seq_len)
        h_b = _pool_prompt(proj, pool_w, pool_scale[l], l, seq_len=seq_len)
        h_c, h_last = _lru_prompt(proj, lru_w, batch=batch, seq_len=seq_len)
        h_d = _mem_attn(proj, mk[None], mv[None], 0, n_seq=batch, rows_per_seq=seq_len, tq=512)
        proj3 = proj.reshape(batch, seq_len, -1)
        outs["pp"].append(proj3[:, seq_len - POOL_BUF:, COL_U * BRANCH_W:(COL_U + 1) * BRANCH_W])
        outs["cp"].append(proj3[:, seq_len - (CONV_W - 1):, COL_X * BRANCH_W:(COL_X + 1) * BRANCH_W])
        xp = _layer_dense_out(xp, h_a, h_b, h_c, h_d, proj, w_branch_bf, w_out, l)
        outs["kp"].append(k_f.reshape(batch, seq_len, 2 * A_HEADS, A_HD))
        outs["vp"].append(v_f.reshape(batch, seq_len, A_HEADS, HEAD_W))
        outs["mk"].append(mk.reshape(batch, n_mem, M_HEADS, M_HD))
        outs["mv"].append(mv.reshape(batch, n_mem, M_HEADS, M_HD))
        outs["hp"].append(h_last.reshape(batch, BRANCH_W))

        proj = _layer_dense_in(xs, ln_g[l], w_in, l)
        k_f, v_f, k_bf, v_bf, q_bf = _rope_call(proj, cos_s, sin_s, seq_len=0, tm=rope_tm)
        h_a = _attn_sample(q_bf, k_bf, v_bf, cache_k, cache_v, l, page_table, proj, subg,
                           lam_vecs, lam_init, n_seq=n_seq, t=t)
        pool_buf = jnp.pad(state_pool[l], ((0, 0), (POOL_PAD - POOL_BUF, 0), (0, 0)))
        h_b = _pool_sample(proj, pool_buf, pool_w, pool_scale[l], l, t=t, past_len=past_len)
        conv_buf = jnp.pad(state_conv[l], ((0, 0), (CONV_PAD - (CONV_W - 1), 0), (0, 0)))
        h0_rows = jnp.repeat(state_h[l], t, axis=0)
        h_c, hs_rows = _lru_sample(proj, conv_buf, h0_rows, lru_w, t=t)
        h_d = _mem_attn(proj, mem_k_s, mem_v_s, l, n_seq=n_seq, rows_per_seq=t, tq=t)
        proj3 = proj.reshape(n_seq, t, -1)
        u_s = proj3[:, :, COL_U * BRANCH_W:(COL_U + 1) * BRANCH_W]
        x_s = proj3[:, :, COL_X * BRANCH_W:(COL_X + 1) * BRANCH_W]
        outs["ps"].append(jnp.concatenate([state_pool[l], u_s], axis=1)[:, -POOL_BUF:])
        outs["cs"].append(jnp.concatenate([state_conv[l], x_s], axis=1)[:, -(CONV_W - 1):])
        xs = _layer_dense_out(xs, h_a, h_b, h_c, h_d, proj, w_branch_bf, w_out, l)
        outs["ks"].append(k_f.reshape(n_seq, t, 2 * A_HEADS, A_HD))
        outs["vs"].append(v_f.reshape(n_seq, t, A_HEADS, HEAD_W))
        outs["hs"].append(hs_rows.reshape(n_seq, t, BRANCH_W)[:, t - 1])

    y_prompt = _rmsnorm(xp, final_g, EPS, F32).reshape(batch, seq_len, D_MODEL)
    y_sample = _rmsnorm(xs, final_g, EPS, F32).reshape(n_seq, t, D_MODEL)
    st = lambda name: jnp.stack(outs[name])
    return (y_prompt, y_sample, st("kp"), st("vp"), st("ks"), st("vs"), st("mk"), st("mv"),
            st("pp"), st("ps"), st("cp"), st("cs"), st("hp"), st("hs"))
```

```python
import functools
import math

import jax
import jax.numpy as jnp
from jax import lax
from jax.experimental import pallas as pl
from jax.experimental.pallas import tpu as pltpu

F32 = jnp.float32
BF16 = jnp.bfloat16

D_MODEL = 2048
BRANCH_W = 1024
A_HEADS = 8
A_HD = 64
HEAD_W = 2 * A_HD
ROPE_THETA = 10000.0
SUBLN_EPS = 1e-5
POOL_WINDOWS = (2, 4, 8, 16)
POOL_GW = BRANCH_W // len(POOL_WINDOWS)
POOL_BUF = max(POOL_WINDOWS) - 1
POOL_PAD = 16
LRU_BLOCKS = 8
LRU_BW = BRANCH_W // LRU_BLOCKS
CONV_W = 4
CONV_PAD = 8
LRU_C = 8.0
M_HEADS = 4
M_HD = BRANCH_W // M_HEADS
N_BRANCH = 4
PAGE_SIZE = 128
EPS = 1e-6
NEG = -1e30
SUBLANES = 8

COL_Q, COL_K, COL_V, COL_U, COL_X, COL_QD = 0, 1, 2, 3, 4, 5
COL_Z = 6
COL_G = 10

VMEM_LIMIT = 56 * 1024 * 1024


def _params(*sem):
    return pltpu.CompilerParams(dimension_semantics=sem, vmem_limit_bytes=VMEM_LIMIT)


def _silu(z):
    return z * jax.nn.sigmoid(z)


def _rmsnorm_kernel(x_ref, g_ref, o_ref, *, eps):
    x = x_ref[...]
    y = x * lax.rsqrt(jnp.mean(x * x, axis=-1, keepdims=True) + eps)
    o_ref[...] = (y * g_ref[...]).astype(o_ref.dtype)


def _rmsnorm(x, g, eps, out_dtype, tm=256):
    m, d = x.shape
    return pl.pallas_call(
        functools.partial(_rmsnorm_kernel, eps=eps),
        out_shape=jax.ShapeDtypeStruct((m, d), out_dtype),
        grid=(m // tm,),
        in_specs=[pl.BlockSpec((tm, d), lambda i: (i, 0)),
                  pl.BlockSpec((1, d), lambda i: (0, 0))],
        out_specs=pl.BlockSpec((tm, d), lambda i: (i, 0)),
        compiler_params=_params("parallel"),
        name="rmsnorm",
    )(x, g.reshape(1, d))


def _mm_kernel(a_ref, w_ref, *rest, has_res):
    if has_res:
        r_ref, o_ref, wb_ref = rest
    else:
        o_ref, wb_ref = rest

    @pl.when(pl.program_id(1) == 0)
    def _():
        wb_ref[...] = w_ref[...].astype(BF16)

    acc = jnp.dot(a_ref[...], wb_ref[...], preferred_element_type=F32)
    if has_res:
        acc = r_ref[...] + acc
    o_ref[...] = acc


def _matmul(a, w, layer, res=None, tm=1024, tn=1024):
    m, k = a.shape
    n = w.shape[2]
    tm = min(tm, m)
    in_specs = [pl.BlockSpec((tm, k), lambda j, i: (i, 0)),
                pl.BlockSpec((None, k, tn), lambda j, i: (layer, 0, j))]
    args = [a, w]
    if res is not None:
        in_specs.append(pl.BlockSpec((tm, tn), lambda j, i: (i, j)))
        args.append(res)
    return pl.pallas_call(
        functools.partial(_mm_kernel, has_res=res is not None),
        out_shape=jax.ShapeDtypeStruct((m, n), F32),
        grid=(n // tn, m // tm),
        in_specs=in_specs,
        out_specs=pl.BlockSpec((tm, tn), lambda j, i: (i, j)),
        scratch_shapes=[pltpu.VMEM((k, tn), BF16)],
        compiler_params=_params("arbitrary", "arbitrary"),
        name="matmul",
    )(*args)


def _rope(x, cos, sin_signed):
    lane = lax.broadcasted_iota(jnp.int32, x.shape, 1)
    first_half = (lane % A_HD) < (A_HD // 2)
    width = x.shape[1]
    partner = jnp.where(first_half,
                        pltpu.roll(x, width - A_HD // 2, 1),
                        pltpu.roll(x, A_HD // 2, 1))
    reps = width // HEAD_W
    return x * jnp.tile(cos, (1, reps)) + partner * jnp.tile(sin_signed, (1, reps))


def _rope_kernel(q_ref, k_ref, v_ref, cos_ref, sin_ref,
                 ko_ref, vo_ref, kb_ref, vb_ref, qo_ref, *, stacked_q):
    cos = cos_ref[...]
    sin = sin_ref[...]
    k = _rope(k_ref[...], cos, sin)
    ko_ref[...] = k
    kb_ref[...] = k.astype(BF16)
    v = v_ref[...]
    vo_ref[...] = v
    vb_ref[...] = v.astype(BF16)
    q = _rope(q_ref[...], cos, sin) * (A_HD ** -0.5)
    if stacked_q:
        tm = q.shape[0]
        lane = lax.broadcasted_iota(jnp.int32, (tm, HEAD_W), 1)
        for h in range(A_HEADS):
            qh = q[:, h * HEAD_W:(h + 1) * HEAD_W]
            qo_ref[h, 0] = jnp.where(lane < A_HD, qh, 0.0).astype(BF16)
            qo_ref[h, 1] = jnp.where(lane >= A_HD, qh, 0.0).astype(BF16)
    else:
        qo_ref[...] = q.astype(BF16)


def _rope_call(proj, cos, sin_signed, *, seq_len, tm):
    m = proj.shape[0]
    stacked = seq_len > 0
    if stacked:
        tps = seq_len // tm
        tab_map = lambda i: (i % tps, 0)
        q_shape = jax.ShapeDtypeStruct((m // seq_len, A_HEADS, 2, seq_len, HEAD_W), BF16)
        q_spec = pl.BlockSpec((None, A_HEADS, 2, tm, HEAD_W),
                              lambda i: (i // tps, 0, 0, i % tps, 0))
    else:
        tab_map = lambda i: (0, 0)
        q_shape = jax.ShapeDtypeStruct((m, BRANCH_W), BF16)
        q_spec = pl.BlockSpec((tm, BRANCH_W), lambda i: (i, 0))
    col = lambda c: pl.BlockSpec((tm, BRANCH_W), lambda i, c=c: (i, c))
    row = pl.BlockSpec((tm, BRANCH_W), lambda i: (i, 0))
    return pl.pallas_call(
        functools.partial(_rope_kernel, stacked_q=stacked),
        out_shape=(jax.ShapeDtypeStruct((m, BRANCH_W), F32),
                   jax.ShapeDtypeStruct((m, BRANCH_W), F32),
                   jax.ShapeDtypeStruct((m, BRANCH_W), BF16),
                   jax.ShapeDtypeStruct((m, BRANCH_W), BF16),
                   q_shape),
        grid=(m // tm,),
        in_specs=[col(COL_Q), col(COL_K), col(COL_V),
                  pl.BlockSpec((tm, HEAD_W), tab_map),
                  pl.BlockSpec((tm, HEAD_W), tab_map)],
        out_specs=(row, row, row, row, q_spec),
        compiler_params=_params("parallel"),
        name="rope",
    )(proj, proj, proj, cos, sin_signed)


def _rope_tables(pos):
    half = A_HD // 2
    inv = 1.0 / (ROPE_THETA ** (jnp.arange(half, dtype=F32) / half))
    ang = pos.astype(F32)[:, None] * inv[None, :]
    cos, sin = jnp.cos(ang), jnp.sin(ang)
    cos = jnp.concatenate([cos, cos], axis=-1)
    sin = jnp.concatenate([-sin, sin], axis=-1)
    return jnp.tile(cos, (1, 2)), jnp.tile(sin, (1, 2))


def _lambda(lam_ref, lam_init):
    lv = lam_ref[...]
    l1 = jnp.sum(lv[0:1] * lv[1:2], axis=-1, keepdims=True)
    l2 = jnp.sum(lv[2:3] * lv[3:4], axis=-1, keepdims=True)
    return jnp.exp(l1) - jnp.exp(l2) + lam_init


def _diff_head_out(o1, o2, lam, subg, z, lam_init):
    d = o1 - lam * o2
    y = d * lax.rsqrt(jnp.mean(d * d, axis=-1, keepdims=True) + SUBLN_EPS) * subg
    return (y * (1.0 - lam_init)) * _silu(z)


_NT = (((1,), (1,)), ((), ()))
LANES = 128


def _lane_block_sum(p):
    out = p[:, :LANES]
    for c in range(1, p.shape[1] // LANES):
        out = out + p[:, c * LANES:(c + 1) * LANES]
    return out


def _softmax_block(s, m_prev):
    m_new = jnp.maximum(m_prev, jnp.max(s, axis=-1, keepdims=True))
    alpha = jnp.exp(m_prev - m_new)
    p = jnp.exp(s - jnp.tile(m_new, (1, s.shape[1] // LANES)))
    return m_new, alpha, p


def _attn_p_kernel(q_ref, k_ref, v_ref, z_ref, subg_ref, lam_ref, o_ref,
                   m_ref, l_ref, acc_ref, *, tq, tk, hpb, lam_init):
    qi = pl.program_id(2)
    rows = 2 * tq
    m_ref[...] = jnp.full(m_ref.shape, NEG, F32)
    l_ref[...] = jnp.zeros(l_ref.shape, F32)
    acc_ref[...] = jnp.zeros(acc_ref.shape, F32)

    def block(start, masked):
        for j in range(hpb):
            cols = slice(j * HEAD_W, (j + 1) * HEAD_W)
            q = q_ref[j].reshape(rows, HEAD_W)
            s = lax.dot_general(q, k_ref[pl.ds(start, tk), cols], _NT,
                                preferred_element_type=F32)
            if masked:
                r = lax.broadcasted_iota(jnp.int32, s.shape, 0)
                c = lax.broadcasted_iota(jnp.int32, s.shape, 1)
                s = jnp.where(start + c <= qi * tq + r % tq, s, NEG)
            m_new, alpha, p = _softmax_block(s, m_ref[j])
            l_ref[j] = alpha * l_ref[j] + _lane_block_sum(p)
            acc_ref[j] = alpha * acc_ref[j] + jnp.dot(
                p.astype(BF16), v_ref[pl.ds(start, tk), cols], preferred_element_type=F32)
            m_ref[j] = m_new

    n_full = (qi * tq) // tk

    def body(ki, carry):
        block(pl.multiple_of(ki * tk, tk), False)
        return carry

    lax.fori_loop(0, n_full, body, 0)
    block(pl.multiple_of(n_full * tk, tk), True)

    lam = _lambda(lam_ref, lam_init)
    for j in range(hpb):
        cols = slice(j * HEAD_W, (j + 1) * HEAD_W)
        o = acc_ref[j] / jnp.sum(l_ref[j], axis=-1, keepdims=True)
        out = _diff_head_out(o[:tq], o[tq:], lam, subg_ref[...], z_ref[:, cols], lam_init)
        o_ref[:, cols] = out.astype(o_ref.dtype)


def _attn_prompt(q_st, k_bf, v_bf, proj, subg, lam_vecs, lam_init, *, batch, seq_len,
                 tq=256, tk=512, hpb=4):
    nq = seq_len // tq
    gw = hpb * HEAD_W
    zcol = COL_Z * (BRANCH_W // gw)
    kv_spec = pl.BlockSpec((seq_len, gw), lambda b, g, qi: (b, g))
    return pl.pallas_call(
        functools.partial(_attn_p_kernel, tq=tq, tk=tk, hpb=hpb, lam_init=lam_init),
        out_shape=jax.ShapeDtypeStruct((batch * seq_len, BRANCH_W), BF16),
        grid=(batch, A_HEADS // hpb, nq),
        in_specs=[pl.BlockSpec((None, hpb, 2, tq, HEAD_W), lambda b, g, qi: (b, g, 0, qi, 0)),
                  kv_spec, kv_spec,
                  pl.BlockSpec((tq, gw), lambda b, g, qi: (b * nq + qi, zcol + g)),
                  pl.BlockSpec((1, HEAD_W), lambda b, g, qi: (0, 0)),
                  pl.BlockSpec((4, A_HD), lambda b, g, qi: (0, 0))],
        out_specs=pl.BlockSpec((tq, gw), lambda b, g, qi: (b * nq + qi, g)),
        scratch_shapes=[pltpu.VMEM((hpb, 2 * tq, LANES), F32),
                        pltpu.VMEM((hpb, 2 * tq, LANES), F32),
                        pltpu.VMEM((hpb, 2 * tq, HEAD_W), F32)],
        compiler_params=_params("parallel", "parallel", "arbitrary"),
        name="attn_prompt",
    )(q_st, k_bf, v_bf, proj, subg, lam_vecs)


def _attn_s_kernel(pt_ref, q_ref, kn_ref, vn_ref, *rest, t, pps, lam_init):
    del pt_ref
    kc_refs, vc_refs = rest[:pps], rest[pps:2 * pps]
    (z_ref, subg_ref, lam_ref, o_ref,
     qbd_ref, knp_ref, vnp_ref, exp_ref, hm_ref, m_ref, l_ref, acc_ref) = rest[2 * pps:]
    p_idx = pl.program_id(1)
    hrows = 2 * t

    def stats(s):
        m_new, alpha, p = _softmax_block(s, m_ref[...])
        l_ref[...] = alpha * l_ref[...] + _lane_block_sum(p)
        m_ref[...] = m_new
        return alpha, p.astype(BF16)

    @pl.when((p_idx == 0) & (pl.program_id(0) == 0))
    def _():
        er = lax.broadcasted_iota(jnp.int32, exp_ref.shape, 0)
        ec = lax.broadcasted_iota(jnp.int32, exp_ref.shape, 1)
        exp_ref[...] = jnp.where(ec // A_HEADS == er, 1.0, 0.0).astype(BF16)
        hr = lax.broadcasted_iota(jnp.int32, hm_ref.shape, 0)
        hc_ = lax.broadcasted_iota(jnp.int32, hm_ref.shape, 1)
        hm_ref[...] = jnp.where(hc_ % A_HEADS == hr // hrows, 1.0, 0.0).astype(BF16)
        knp_ref[...] = jnp.zeros(knp_ref.shape, BF16)
        vnp_ref[...] = jnp.zeros(vnp_ref.shape, BF16)

    @pl.when(p_idx == 0)
    def _():
        q = q_ref[...]
        lane = lax.broadcasted_iota(jnp.int32, q.shape, 1)
        for hc in range(2 * A_HEADS):
            keep = (lane >= hc * A_HD) & (lane < (hc + 1) * A_HD)
            qbd_ref[hc * t:(hc + 1) * t, :] = jnp.where(keep, q, jnp.zeros_like(q))
        m_ref[...] = jnp.full(m_ref.shape, NEG, F32)
        l_ref[...] = jnp.zeros(l_ref.shape, F32)
        knp_ref[0:t, :] = kn_ref[...]
        vnp_ref[0:t, :] = vn_ref[...]
        s = lax.dot_general(qbd_ref[...], knp_ref[...], _NT, preferred_element_type=F32)
        r = lax.broadcasted_iota(jnp.int32, s.shape, 0)
        c = lax.broadcasted_iota(jnp.int32, s.shape, 1)
        s = jnp.where(c <= (r % t), s, NEG)
        _, pb = stats(s)
        acc_ref[...] = jnp.concatenate(
            [jnp.dot(pb[h * hrows:(h + 1) * hrows], vnp_ref[:, h * HEAD_W:(h + 1) * HEAD_W],
                     preferred_element_type=F32) for h in range(A_HEADS)], axis=0)

    kcat = jnp.concatenate(
        [r[...].reshape(BRANCH_W, PAGE_SIZE).astype(BF16) for r in kc_refs], axis=1)
    s = jnp.dot(qbd_ref[...], kcat, preferred_element_type=F32)
    alpha, pb = stats(s)
    pv = None
    for i, vc_ref in enumerate(vc_refs):
        spread = jnp.dot(pb[:, i * PAGE_SIZE:(i + 1) * PAGE_SIZE], exp_ref[...],
                         preferred_element_type=F32).astype(BF16)
        vc = vc_ref[...].reshape(PAGE_SIZE * A_HEADS, HEAD_W).astype(BF16)
        part = jnp.dot(spread * hm_ref[...], vc, preferred_element_type=F32)
        pv = part if pv is None else pv + part
    acc_ref[...] = alpha * acc_ref[...] + pv

    @pl.when(p_idx == pl.num_programs(1) - 1)
    def _():
        o = acc_ref[...] / jnp.sum(l_ref[...], axis=-1, keepdims=True)
        lam = _lambda(lam_ref, lam_init)
        subg = subg_ref[...]
        for h in range(A_HEADS):
            o1 = o[(2 * h) * t:(2 * h + 1) * t]
            o2 = o[(2 * h + 1) * t:(2 * h + 2) * t]
            zh = z_ref[:, h * HEAD_W:(h + 1) * HEAD_W]
            out = _diff_head_out(o1, o2, lam, subg, zh, lam_init)
            o_ref[:, h * HEAD_W:(h + 1) * HEAD_W] = out.astype(o_ref.dtype)


def _attn_sample(q_bf, k_bf, v_bf, cache_k, cache_v, layer, page_table, proj, subg, lam_vecs,
                 lam_init, *, n_seq, t, pps=8):
    n_pages = page_table.shape[1]
    rows = 2 * A_HEADS * t
    pt_flat = page_table.reshape(-1)
    kc = jnp.transpose(cache_k, (0, 1, 3, 4, 2))
    seq_spec = pl.BlockSpec((t, BRANCH_W), lambda b, p, pt: (b, 0))

    def page(i):
        return lambda b, p, pt: (layer, pt[b * n_pages + p * pps + i], 0, 0, 0)

    k_specs = [pl.BlockSpec((None, None, 2 * A_HEADS, A_HD, PAGE_SIZE), page(i))
               for i in range(pps)]
    v_specs = [pl.BlockSpec((None, None, PAGE_SIZE, A_HEADS, HEAD_W), page(i))
               for i in range(pps)]
    grid_spec = pltpu.PrefetchScalarGridSpec(
        num_scalar_prefetch=1,
        grid=(n_seq, n_pages // pps),
        in_specs=[seq_spec, seq_spec, seq_spec] + k_specs + v_specs + [
            pl.BlockSpec((t, BRANCH_W), lambda b, p, pt: (b, COL_Z)),
            pl.BlockSpec((1, HEAD_W), lambda b, p, pt: (0, 0)),
            pl.BlockSpec((4, A_HD), lambda b, p, pt: (0, 0))],
        out_specs=pl.BlockSpec((t, BRANCH_W), lambda b, p, pt: (b, 0)),
        scratch_shapes=[pltpu.VMEM((rows, BRANCH_W), BF16),
                        pltpu.VMEM((PAGE_SIZE, BRANCH_W), BF16),
                        pltpu.VMEM((PAGE_SIZE, BRANCH_W), BF16),
                        pltpu.VMEM((PAGE_SIZE, PAGE_SIZE * A_HEADS), BF16),
                        pltpu.VMEM((rows, PAGE_SIZE * A_HEADS), BF16),
                        pltpu.VMEM((rows, LANES), F32), pltpu.VMEM((rows, LANES), F32),
                        pltpu.VMEM((rows, HEAD_W), F32)],
    )
    return pl.pallas_call(
        functools.partial(_attn_s_kernel, t=t, pps=pps, lam_init=lam_init),
        out_shape=jax.ShapeDtypeStruct((n_seq * t, BRANCH_W), BF16),
        grid_spec=grid_spec,
        compiler_params=_params("arbitrary", "arbitrary"),
        name="attn_sample",
    )(pt_flat, q_bf, k_bf, v_bf, *([kc] * pps), *([cache_v] * pps), proj, subg, lam_vecs)


def _window_sum(e, log2_win):
    s = e
    for k in range(log2_win):
        s = s + pltpu.roll(s, 1 << k, 0)
    return s


def _pool_groups(ext, pos, w_ref, scale_ref, z, o_ref, pick):
    for gi, win in enumerate(POOL_WINDOWS):
        c0, c1 = gi * POOL_GW, (gi + 1) * POOL_GW
        e = ext[:, c0:c1]
        cur = pick(e)
        tot = pick(_window_sum(e, gi + 1))
        cnt = jnp.minimum(pos + 1, win).astype(F32)
        mixed = tot / cnt - cur
        out = jnp.dot(mixed.astype(BF16), w_ref[gi].astype(BF16), preferred_element_type=F32)
        out = out * scale_ref[:, c0:c1]
        o_ref[:, c0:c1] = (out * _silu(z[:, c0:c1])).astype(o_ref.dtype)


def _pool_p_kernel(u_ref, prev_ref, z_ref, w_ref, scale_ref, o_ref, *, tm, tps):
    i = pl.program_id(0)
    first = (i % tps) == 0
    prev = jnp.where(first, 0.0, prev_ref[...])
    ext = jnp.concatenate([prev, u_ref[...]], axis=0)
    pos = (i % tps) * tm + lax.broadcasted_iota(jnp.int32, (tm, POOL_GW), 0)
    _pool_groups(ext, pos, w_ref, scale_ref, z_ref[...], o_ref, lambda x: x[POOL_PAD:])


def _pool_s_kernel(u_ref, buf_ref, z_ref, w_ref, scale_ref, o_ref, *, nb, t, past_len):
    u3 = u_ref[...].reshape(nb, t, BRANCH_W)
    ext = jnp.concatenate([buf_ref[...], u3], axis=1)
    ext = ext.reshape(nb * (POOL_PAD + t), BRANCH_W)
    pos = past_len + lax.broadcasted_iota(jnp.int32, (nb * t, POOL_GW), 0) % t

    def pick(x):
        x3 = x.reshape(nb, POOL_PAD + t, x.shape[-1])
        return x3[:, POOL_PAD:].reshape(nb * t, x.shape[-1])

    _pool_groups(ext, pos, w_ref, scale_ref, z_ref[...], o_ref, pick)


def _pool_prompt(proj, pool_w, pool_scale, layer, *, seq_len, tm=256):
    m = proj.shape[0]
    tps = seq_len // tm
    per = tm // POOL_PAD
    return pl.pallas_call(
        functools.partial(_pool_p_kernel, tm=tm, tps=tps),
        out_shape=jax.ShapeDtypeStruct((m, BRANCH_W), BF16),
        grid=(m // tm,),
        in_specs=[pl.BlockSpec((tm, BRANCH_W), lambda i: (i, COL_U)),
                  pl.BlockSpec((POOL_PAD, BRANCH_W),
                               lambda i: (jnp.maximum(i * per - 1, 0), COL_U)),
                  pl.BlockSpec((tm, BRANCH_W), lambda i: (i, COL_Z + 1)),
                  pl.BlockSpec((None,) + pool_w.shape[1:], lambda i: (layer, 0, 0, 0)),
                  pl.BlockSpec((1, BRANCH_W), lambda i: (0, 0))],
        out_specs=pl.BlockSpec((tm, BRANCH_W), lambda i: (i, 0)),
        compiler_params=_params("parallel"),
        name="pool_prompt",
    )(proj, proj, proj, pool_w, pool_scale.reshape(1, BRANCH_W))


def _pool_sample(proj, buf_padded, pool_w, pool_scale, layer, *, t, past_len, nb=32):
    m = proj.shape[0]
    return pl.pallas_call(
        functools.partial(_pool_s_kernel, nb=nb, t=t, past_len=past_len),
        out_shape=jax.ShapeDtypeStruct((m, BRANCH_W), BF16),
        grid=(m // (nb * t),),
        in_specs=[pl.BlockSpec((nb * t, BRANCH_W), lambda i: (i, COL_U)),
                  pl.BlockSpec((nb, POOL_PAD, BRANCH_W), lambda i: (i, 0, 0)),
                  pl.BlockSpec((nb * t, BRANCH_W), lambda i: (i, COL_Z + 1)),
                  pl.BlockSpec((None,) + pool_w.shape[1:], lambda i: (layer, 0, 0, 0)),
                  pl.BlockSpec((1, BRANCH_W), lambda i: (0, 0))],
        out_specs=pl.BlockSpec((nb * t, BRANCH_W), lambda i: (i, 0)),
        compiler_params=_params("parallel"),
        name="pool_sample",
    )(proj, buf_padded, proj, pool_w, pool_scale.reshape(1, BRANCH_W))


def _conv_taps(ext, pick, cw_ref, cb_ref):
    out = cb_ref[...] + pick(ext) * cw_ref[CONV_W - 1:CONV_W, :]
    for k in range(CONV_W - 1):
        shift = CONV_W - 1 - k
        out = out + pick(pltpu.roll(ext, shift, 0)) * cw_ref[k:k + 1, :]
    return out


def _block_diag_gate(xb, w_ref, b_ref):
    parts = [jnp.dot(xb[:, n * LRU_BW:(n + 1) * LRU_BW], w_ref[n].astype(BF16),
                     preferred_element_type=F32) for n in range(LRU_BLOCKS)]
    return jax.nn.sigmoid(jnp.concatenate(parts, axis=-1) + b_ref[...])


def _lru_coeffs(xc, wa_ref, ba_ref, wx_ref, bx_ref, lam_ref):
    xb = xc.astype(BF16)
    r = _block_diag_gate(xb, wa_ref, ba_ref)
    i = _block_diag_gate(xb, wx_ref, bx_ref)
    nl = -lam_ref[...]
    softplus = jnp.maximum(nl, 0.0) + jnp.log1p(jnp.exp(-jnp.abs(nl)))
    log_a = -LRU_C * r * softplus
    a = jnp.exp(log_a)
    b = jnp.sqrt(1.0 - jnp.exp(2.0 * log_a)) * (i * xc)
    return a, b


def _group_scan(a, b):
    shape = a.shape
    grouped = (shape[0] // SUBLANES, SUBLANES, shape[1])
    a = a.reshape(grouped)
    b = b.reshape(grouped)
    row = lax.broadcasted_iota(jnp.int32, grouped, 1)
    d = 1
    while d < SUBLANES:
        valid = row >= d
        a_sh = pltpu.roll(a, d, 1)
        b_sh = pltpu.roll(b, d, 1)
        b = jnp.where(valid, a * b_sh + b, b)
        a = jnp.where(valid, a * a_sh, a)
        d *= 2
    return a.reshape(shape), b.reshape(shape)


def _lru_p_kernel(x_ref, prev_ref, z_ref, cw_ref, cb_ref, wa_ref, ba_ref, wx_ref, bx_ref,
                  lam_ref, o_ref, hl_ref, a_s, b_s, h_s, *, tm, tps):
    i = pl.program_id(0)
    first = (i % tps) == 0

    @pl.when(first)
    def _():
        h_s[...] = jnp.zeros(h_s.shape, F32)

    prev = jnp.where(first, 0.0, prev_ref[...])
    ext = jnp.concatenate([prev, x_ref[...]], axis=0)
    xc = _conv_taps(ext, lambda x: x[CONV_PAD:], cw_ref, cb_ref)
    a, b = _lru_coeffs(xc, wa_ref, ba_ref, wx_ref, bx_ref, lam_ref)
    a, b = _group_scan(a, b)
    a_s[...] = a
    b_s[...] = b

    def body(g, h):
        r0 = pl.multiple_of(g * SUBLANES, SUBLANES)
        hs = a_s[pl.ds(r0, SUBLANES), :] * h + b_s[pl.ds(r0, SUBLANES), :]
        b_s[pl.ds(r0, SUBLANES), :] = hs
        return hs[SUBLANES - 1:SUBLANES, :]

    h = lax.fori_loop(0, tm // SUBLANES, body, h_s[...])
    h_s[...] = h
    hl_ref[...] = h
    o_ref[...] = (b_s[...] * _silu(z_ref[...])).astype(o_ref.dtype)


def _lru_s_kernel(x_ref, buf_ref, h0_ref, z_ref, cw_ref, cb_ref, wa_ref, ba_ref, wx_ref, bx_ref,
                  lam_ref, o_ref, hs_ref, *, nb, t):
    x3 = x_ref[...].reshape(nb, t, BRANCH_W)
    ext = jnp.concatenate([buf_ref[...], x3], axis=1).reshape(nb * (CONV_PAD + t), BRANCH_W)

    def pick(x):
        return x.reshape(nb, CONV_PAD + t, BRANCH_W)[:, CONV_PAD:].reshape(nb * t, BRANCH_W)

    xc = _conv_taps(ext, pick, cw_ref, cb_ref)
    a, b = _lru_coeffs(xc, wa_ref, ba_ref, wx_ref, bx_ref, lam_ref)
    a, b = _group_scan(a, b)
    hs = a * h0_ref[...] + b
    hs_ref[...] = hs
    o_ref[...] = (hs * _silu(z_ref[...])).astype(o_ref.dtype)


def _lru_weight_specs(nargs_map):
    full = lambda shape: pl.BlockSpec(shape, lambda *a: (0,) * len(shape))
    del nargs_map
    return [full((CONV_W, BRANCH_W)), full((1, BRANCH_W)),
            full((LRU_BLOCKS, LRU_BW, LRU_BW)), full((1, BRANCH_W)),
            full((LRU_BLOCKS, LRU_BW, LRU_BW)), full((1, BRANCH_W)),
            full((1, BRANCH_W))]


def _lru_weights(conv_w, conv_b, wa, ba, wx, bx, lam):
    r = lambda v: v.reshape(1, BRANCH_W)
    return (conv_w, r(conv_b), wa, r(ba), wx, r(bx), r(lam))


def _lru_prompt(proj, weights, *, batch, seq_len, tm=256):
    m = proj.shape[0]
    tps = seq_len // tm
    per = tm // CONV_PAD
    return pl.pallas_call(
        functools.partial(_lru_p_kernel, tm=tm, tps=tps),
        out_shape=(jax.ShapeDtypeStruct((m, BRANCH_W), BF16),
                   jax.ShapeDtypeStruct((batch, 1, BRANCH_W), F32)),
        grid=(m // tm,),
        in_specs=[pl.BlockSpec((tm, BRANCH_W), lambda i: (i, COL_X)),
                  pl.BlockSpec((CONV_PAD, BRANCH_W),
                               lambda i: (jnp.maximum(i * per - 1, 0), COL_X)),
                  pl.BlockSpec((tm, BRANCH_W), lambda i: (i, COL_Z + 2))]
        + _lru_weight_specs(None),
        out_specs=(pl.BlockSpec((tm, BRANCH_W), lambda i: (i, 0)),
                   pl.BlockSpec((None, 1, BRANCH_W), lambda i: (i // tps, 0, 0))),
        scratch_shapes=[pltpu.VMEM((tm, BRANCH_W), F32), pltpu.VMEM((tm, BRANCH_W), F32),
                        pltpu.VMEM((1, BRANCH_W), F32)],
        compiler_params=_params("arbitrary"),
        name="lru_prompt",
    )(proj, proj, proj, *weights)


def _lru_sample(proj, buf_padded, h0_rows, weights, *, t, nb=32):
    m = proj.shape[0]
    rows = nb * t
    return pl.pallas_call(
        functools.partial(_lru_s_kernel, nb=nb, t=t),
        out_shape=(jax.ShapeDtypeStruct((m, BRANCH_W), BF16),
                   jax.ShapeDtypeStruct((m, BRANCH_W), F32)),
        grid=(m // rows,),
        in_specs=[pl.BlockSpec((rows, BRANCH_W), lambda i: (i, COL_X)),
                  pl.BlockSpec((nb, CONV_PAD, BRANCH_W), lambda i: (i, 0, 0)),
                  pl.BlockSpec((rows, BRANCH_W), lambda i: (i, 0)),
                  pl.BlockSpec((rows, BRANCH_W), lambda i: (i, COL_Z + 2))]
        + _lru_weight_specs(None),
        out_specs=(pl.BlockSpec((rows, BRANCH_W), lambda i: (i, 0)),
                   pl.BlockSpec((rows, BRANCH_W), lambda i: (i, 0))),
        compiler_params=_params("parallel"),
        name="lru_sample",
    )(proj, buf_padded, h0_rows, proj, *weights)


def _mem_attn_kernel(q_ref, k_ref, v_ref, z_ref, o_ref):
    q = q_ref[...].astype(BF16)
    k = k_ref[...].astype(BF16)
    v = v_ref[...].astype(BF16)
    z = z_ref[...]
    for h in range(M_HEADS):
        c0, c1 = h * M_HD, (h + 1) * M_HD
        s = lax.dot_general(q[:, c0:c1], k[:, c0:c1], _NT, preferred_element_type=F32)
        s = s * (M_HD ** -0.5)
        p = jnp.exp(s - jnp.max(s, axis=-1, keepdims=True))
        den = jnp.sum(p, axis=-1, keepdims=True)
        o = jnp.dot((p / den).astype(BF16), v[:, c0:c1], preferred_element_type=F32)
        o_ref[:, c0:c1] = (o * _silu(z[:, c0:c1])).astype(o_ref.dtype)


def _mem_attn(proj, mem_k, mem_v, layer, *, n_seq, rows_per_seq, tq):
    n_mem = mem_k.shape[1] // n_seq
    nq = rows_per_seq // tq
    kv_spec = pl.BlockSpec((None, n_mem, BRANCH_W), lambda b, qi: (layer, b, 0))
    return pl.pallas_call(
        _mem_attn_kernel,
        out_shape=jax.ShapeDtypeStruct((n_seq * rows_per_seq, BRANCH_W), BF16),
        grid=(n_seq, nq),
        in_specs=[pl.BlockSpec((tq, BRANCH_W), lambda b, qi: (b * nq + qi, COL_QD)),
                  kv_spec, kv_spec,
                  pl.BlockSpec((tq, BRANCH_W), lambda b, qi: (b * nq + qi, COL_Z + 3))],
        out_specs=pl.BlockSpec((tq, BRANCH_W), lambda b, qi: (b * nq + qi, 0)),
        compiler_params=_params("parallel", "parallel"),
        name="mem_attn",
    )(proj, mem_k, mem_v, proj)


def _mem_attn_s_kernel(q_ref, k_ref, v_ref, z_ref, o_ref, *, nb, t):
    n_mem = k_ref.shape[1]
    rows = M_HEADS * t
    col = lax.broadcasted_iota(jnp.int32, (rows, n_mem * M_HEADS), 1)
    row = lax.broadcasted_iota(jnp.int32, (rows, n_mem * M_HEADS), 0)
    own_head = (col % M_HEADS) == (row // t)
    for b in range(nb):
        q = q_ref[b * t:(b + 1) * t, :].astype(BF16)
        qs = jnp.concatenate([q[:, h * M_HD:(h + 1) * M_HD] for h in range(M_HEADS)], axis=0)
        k2 = k_ref[b].reshape(n_mem * M_HEADS, M_HD).astype(BF16)
        v2 = v_ref[b].reshape(n_mem * M_HEADS, M_HD).astype(BF16)
        s = lax.dot_general(qs, k2, _NT, preferred_element_type=F32) * (M_HD ** -0.5)
        s = jnp.where(own_head, s, NEG)
        p = jnp.exp(s - jnp.max(s, axis=-1, keepdims=True))
        den = jnp.sum(p, axis=-1, keepdims=True)
        o = jnp.dot((p / den).astype(BF16), v2, preferred_element_type=F32)
        for h in range(M_HEADS):
            c0, c1 = h * M_HD, (h + 1) * M_HD
            zh = z_ref[b * t:(b + 1) * t, c0:c1]
            o_ref[b * t:(b + 1) * t, c0:c1] = (o[h * t:(h + 1) * t] * _silu(zh)).astype(o_ref.dtype)


def _mem_attn_sample(proj, cache_mem_k, cache_mem_v, layer, *, t, nb=4):
    _, n_seq, n_mem, _, _ = cache_mem_k.shape
    kv_spec = pl.BlockSpec((None, nb, n_mem, M_HEADS, M_HD), lambda i: (layer, i, 0, 0, 0))
    return pl.pallas_call(
        functools.partial(_mem_attn_s_kernel, nb=nb, t=t),
        out_shape=jax.ShapeDtypeStruct((n_seq * t, BRANCH_W), BF16),
        grid=(n_seq // nb,),
        in_specs=[pl.BlockSpec((nb * t, BRANCH_W), lambda i: (i, COL_QD)),
                  kv_spec, kv_spec,
                  pl.BlockSpec((nb * t, BRANCH_W), lambda i: (i, COL_Z + 3))],
        out_specs=pl.BlockSpec((nb * t, BRANCH_W), lambda i: (i, 0)),
        compiler_params=_params("parallel"),
        name="mem_attn_sample",
    )(proj, cache_mem_k, cache_mem_v, proj)


def _merge_kernel(*refs):
    h_refs, g_refs = refs[:N_BRANCH], refs[N_BRANCH:2 * N_BRANCH]
    w_ref, o_ref = refs[2 * N_BRANCH:]
    acc = None
    for n in range(N_BRANCH):
        up = jnp.dot(h_refs[n][...], w_ref[n], preferred_element_type=F32)
        term = jax.nn.sigmoid(g_refs[n][...]) * up
        acc = term if acc is None else acc + term
    o_ref[...] = acc.astype(o_ref.dtype)


def _merge(h_a, h_b, h_c, h_d, proj, w_branch_bf, layer, tm=1024, tn=512):
    m = h_a.shape[0]
    tm = min(tm, m)
    h_spec = pl.BlockSpec((tm, BRANCH_W), lambda i, j: (i, 0))
    gate_col0 = COL_G * BRANCH_W // tn
    g_specs = [pl.BlockSpec((tm, tn), lambda i, j, n=n: (i, gate_col0 + n * (D_MODEL // tn) + j))
               for n in range(N_BRANCH)]
    return pl.pallas_call(
        _merge_kernel,
        out_shape=jax.ShapeDtypeStruct((m, D_MODEL), BF16),
        grid=(m // tm, D_MODEL // tn),
        in_specs=[h_spec] * N_BRANCH + g_specs + [
            pl.BlockSpec((None, N_BRANCH, BRANCH_W, tn), lambda i, j: (layer, 0, 0, j))],
        out_specs=pl.BlockSpec((tm, tn), lambda i, j: (i, j)),
        compiler_params=_params("parallel", "parallel"),
        name="merge",
    )(h_a, h_b, h_c, h_d, proj, proj, proj, proj, w_branch_bf)


def _layer_dense_in(x, ln_g, w_in, layer):
    xn = _rmsnorm(x, ln_g, EPS, BF16)
    return _matmul(xn, w_in, layer)


def _layer_dense_out(x, h_a, h_b, h_c, h_d, proj, w_branch_bf, w_out, layer):
    merged = _merge(h_a, h_b, h_c, h_d, proj, w_branch_bf, layer)
    return _matmul(merged, w_out, layer, res=x, tn=512)


def kernel(x_prompt, x_sample, cache_k, cache_v, cache_mem_k, cache_mem_v, state_pool, state_conv, state_h, page_table, mem_prompt, ln_g, w_in, lam_q1, lam_k1, lam_q2, lam_k2, attn_sub_g, pool_w, pool_scale, conv_w, conv_b, lru_wa, lru_ba, lru_wx, lru_bx, lru_lambda, mem_g, w_mem_k, w_mem_v, w_branch, w_out, final_g):
    batch, seq_len, _ = x_prompt.shape
    n_seq, t, _ = x_sample.shape
    depth = w_in.shape[0]
    n_mem = mem_prompt.shape[1]
    past_len = page_table.shape[1] * PAGE_SIZE
    rope_tm = 256

    cos_p, sin_p = _rope_tables(jnp.arange(seq_len, dtype=jnp.int32))
    cos_s, sin_s = _rope_tables(past_len + jnp.arange(t, dtype=jnp.int32))
    cos_s = jnp.tile(cos_s, (rope_tm // t, 1))
    sin_s = jnp.tile(sin_s, (rope_tm // t, 1))

    xp = x_prompt.reshape(batch * seq_len, D_MODEL)
    xs = x_sample.reshape(n_seq * t, D_MODEL)
    mem_rows = mem_prompt.reshape(batch * n_mem, D_MODEL)

    w_branch_bf = w_branch.astype(BF16)

    outs = {name: [] for name in ("kp", "vp", "ks", "vs", "mk", "mv", "pp", "ps", "cp", "cs", "hp", "hs")}
    for l in range(depth):
        lam_init = 0.8 - 0.6 * math.exp(-0.3 * l)
        lam_vecs = jnp.stack([lam_q1[l], lam_k1[l], lam_q2[l], lam_k2[l]])
        subg = attn_sub_g[l].reshape(1, HEAD_W)
        lru_w = _lru_weights(conv_w[l], conv_b[l], lru_wa[l], lru_ba[l], lru_wx[l], lru_bx[l],
                             lru_lambda[l])

        mn = _rmsnorm(mem_rows, mem_g[l], EPS, BF16)
        mk = _matmul(mn, w_mem_k, l)
        mv = _matmul(mn, w_mem_v, l)
        proj = _layer_dense_in(xp, ln_g[l], w_in, l)
        k_f, v_f, k_bf, v_bf, q_st = _rope_call(proj, cos_p, sin_p, seq_len=seq_len, tm=rope_tm)
        h_a = _attn_prompt(q_st, k_bf, v_bf, proj, subg, lam_vecs, lam_init,
                           batch=batch, seq_len=seq_len)
        h_b = _pool_prompt(proj, pool_w, pool_scale[l], l, seq_len=seq_len)
        h_c, h_last = _lru_prompt(proj, lru_w, batch=batch, seq_len=seq_len)
        h_d = _mem_attn(proj, mk[None], mv[None], 0, n_seq=batch, rows_per_seq=seq_len, tq=512)
        proj3 = proj.reshape(batch, seq_len, -1)
        outs["pp"].append(proj3[:, seq_len - POOL_BUF:, COL_U * BRANCH_W:(COL_U + 1) * BRANCH_W])
        outs["cp"].append(proj3[:, seq_len - (CONV_W - 1):, COL_X * BRANCH_W:(COL_X + 1) * BRANCH_W])
        xp = _layer_dense_out(xp, h_a, h_b, h_c, h_d, proj, w_branch_bf, w_out, l)
        outs["kp"].append(k_f.reshape(batch, seq_len, 2 * A_HEADS, A_HD))
        outs["vp"].append(v_f.reshape(batch, seq_len, A_HEADS, HEAD_W))
        outs["mk"].append(mk.reshape(batch, n_mem, M_HEADS, M_HD))
        outs["mv"].append(mv.reshape(batch, n_mem, M_HEADS, M_HD))
        outs["hp"].append(h_last.reshape(batch, BRANCH_W))

        proj = _layer_dense_in(xs, ln_g[l], w_in, l)
        k_f, v_f, k_bf, v_bf, q_bf = _rope_call(proj, cos_s, sin_s, seq_len=0, tm=rope_tm)
        h_a = _attn_sample(q_bf, k_bf, v_bf, cache_k, cache_v, l, page_table, proj, subg,
                           lam_vecs, lam_init, n_seq=n_seq, t=t)
        pool_buf = jnp.pad(state_pool[l], ((0, 0), (POOL_PAD - POOL_BUF, 0), (0, 0)))
        h_b = _pool_sample(proj, pool_buf, pool_w, pool_scale[l], l, t=t, past_len=past_len)
        conv_buf = jnp.pad(state_conv[l], ((0, 0), (CONV_PAD - (CONV_W - 1), 0), (0, 0)))
        h0_rows = jnp.repeat(state_h[l], t, axis=0)
        h_c, hs_rows = _lru_sample(proj, conv_buf, h0_rows, lru_w, t=t)
        h_d = _mem_attn_sample(proj, cache_mem_k, cache_mem_v, l, t=t)
        proj3 = proj.reshape(n_seq, t, -1)
        u_s = proj3[:, :, COL_U * BRANCH_W:(COL_U + 1) * BRANCH_W]
        x_s = proj3[:, :, COL_X * BRANCH_W:(COL_X + 1) * BRANCH_W]
        outs["ps"].append(jnp.concatenate([state_pool[l], u_s], axis=1)[:, -POOL_BUF:])
        outs["cs"].append(jnp.concatenate([state_conv[l], x_s], axis=1)[:, -(CONV_W - 1):])
        xs = _layer_dense_out(xs, h_a, h_b, h_c, h_d, proj, w_branch_bf, w_out, l)
        outs["ks"].append(k_f.reshape(n_seq, t, 2 * A_HEADS, A_HD))
        outs["vs"].append(v_f.reshape(n_seq, t, A_HEADS, HEAD_W))
        outs["hs"].append(hs_rows.reshape(n_seq, t, BRANCH_W)[:, t - 1])

    y_prompt = _rmsnorm(xp, final_g, EPS, F32).reshape(batch, seq_len, D_MODEL)
    y_sample = _rmsnorm(xs, final_g, EPS, F32).reshape(n_seq, t, D_MODEL)
    st = lambda name: jnp.stack(outs[name])
    return (y_prompt, y_sample, st("kp"), st("vp"), st("ks"), st("vs"), st("mk"), st("mv"),
            st("pp"), st("ps"), st("cp"), st("cs"), st("hp"), st("hs"))
```

```python
import functools
import math

import jax
import jax.numpy as jnp
from jax import lax
from jax.experimental import pallas as pl
from jax.experimental.pallas import tpu as pltpu

F32 = jnp.float32
BF16 = jnp.bfloat16

D_MODEL = 2048
BRANCH_W = 1024
A_HEADS = 8
A_HD = 64
HEAD_W = 2 * A_HD
ROPE_THETA = 10000.0
SUBLN_EPS = 1e-5
POOL_WINDOWS = (2, 4, 8, 16)
POOL_GW = BRANCH_W // len(POOL_WINDOWS)
POOL_BUF = max(POOL_WINDOWS) - 1
POOL_PAD = 16
LRU_BLOCKS = 8
LRU_BW = BRANCH_W // LRU_BLOCKS
CONV_W = 4
CONV_PAD = 8
LRU_C = 8.0
M_HEADS = 4
M_HD = BRANCH_W // M_HEADS
N_BRANCH = 4
PAGE_SIZE = 128
EPS = 1e-6
NEG = -1e30
SUBLANES = 8
LOG2_E = math.log2(math.e)

COL_Q, COL_K, COL_V, COL_U, COL_X, COL_QD = 0, 1, 2, 3, 4, 5
COL_Z = 6
COL_G = 10

VMEM_LIMIT = 56 * 1024 * 1024


def _params(*sem):
    return pltpu.CompilerParams(dimension_semantics=sem, vmem_limit_bytes=VMEM_LIMIT)


def _silu(z):
    return z * jax.nn.sigmoid(z)


def _rmsnorm_kernel(x_ref, g_ref, o_ref, *, eps):
    x = x_ref[...]
    y = x * lax.rsqrt(jnp.mean(x * x, axis=-1, keepdims=True) + eps)
    o_ref[...] = (y * g_ref[...]).astype(o_ref.dtype)


def _rmsnorm(x, g, eps, out_dtype, tm=256):
    m, d = x.shape
    return pl.pallas_call(
        functools.partial(_rmsnorm_kernel, eps=eps),
        out_shape=jax.ShapeDtypeStruct((m, d), out_dtype),
        grid=(m // tm,),
        in_specs=[pl.BlockSpec((tm, d), lambda i: (i, 0)),
                  pl.BlockSpec((1, d), lambda i: (0, 0))],
        out_specs=pl.BlockSpec((tm, d), lambda i: (i, 0)),
        compiler_params=_params("parallel"),
        name="rmsnorm",
    )(x, g.reshape(1, d))


def _mm_kernel(a_ref, w_ref, *rest, has_res):
    if has_res:
        r_ref, o_ref, wb_ref = rest
    else:
        o_ref, wb_ref = rest

    @pl.when(pl.program_id(1) == 0)
    def _():
        wb_ref[...] = w_ref[...].astype(BF16)

    acc = jnp.dot(a_ref[...], wb_ref[...], preferred_element_type=F32)
    if has_res:
        acc = r_ref[...] + acc
    o_ref[...] = acc


def _matmul(a, w, layer, res=None, tm=1024, tn=1024):
    m, k = a.shape
    n = w.shape[2]
    tm = min(tm, m)
    in_specs = [pl.BlockSpec((tm, k), lambda j, i: (i, 0)),
                pl.BlockSpec((None, k, tn), lambda j, i: (layer, 0, j))]
    args = [a, w]
    if res is not None:
        in_specs.append(pl.BlockSpec((tm, tn), lambda j, i: (i, j)))
        args.append(res)
    return pl.pallas_call(
        functools.partial(_mm_kernel, has_res=res is not None),
        out_shape=jax.ShapeDtypeStruct((m, n), F32),
        grid=(n // tn, m // tm),
        in_specs=in_specs,
        out_specs=pl.BlockSpec((tm, tn), lambda j, i: (i, j)),
        scratch_shapes=[pltpu.VMEM((k, tn), BF16)],
        compiler_params=_params("arbitrary", "arbitrary"),
        name="matmul",
    )(*args)


def _rope(x, cos, sin_signed):
    lane = lax.broadcasted_iota(jnp.int32, x.shape, 1)
    first_half = (lane % A_HD) < (A_HD // 2)
    width = x.shape[1]
    partner = jnp.where(first_half,
                        pltpu.roll(x, width - A_HD // 2, 1),
                        pltpu.roll(x, A_HD // 2, 1))
    reps = width // HEAD_W
    return x * jnp.tile(cos, (1, reps)) + partner * jnp.tile(sin_signed, (1, reps))


def _rope_kernel(q_ref, k_ref, v_ref, cos_ref, sin_ref,
                 ko_ref, vo_ref, kb_ref, vb_ref, qo_ref, *, stacked_q):
    cos = cos_ref[...]
    sin = sin_ref[...]
    k = _rope(k_ref[...], cos, sin)
    ko_ref[...] = k
    kb_ref[...] = k.astype(BF16)
    v = v_ref[...]
    vo_ref[...] = v
    vb_ref[...] = v.astype(BF16)
    q = _rope(q_ref[...], cos, sin) * (A_HD ** -0.5 * LOG2_E)
    if stacked_q:
        tm = q.shape[0]
        lane = lax.broadcasted_iota(jnp.int32, (tm, HEAD_W), 1)
        for h in range(A_HEADS):
            qh = q[:, h * HEAD_W:(h + 1) * HEAD_W]
            qo_ref[h, 0] = jnp.where(lane < A_HD, qh, 0.0).astype(BF16)
            qo_ref[h, 1] = jnp.where(lane >= A_HD, qh, 0.0).astype(BF16)
    else:
        qo_ref[...] = q.astype(BF16)


def _rope_call(proj, cos, sin_signed, *, seq_len, tm):
    m = proj.shape[0]
    stacked = seq_len > 0
    if stacked:
        tps = seq_len // tm
        tab_map = lambda i: (i % tps, 0)
        q_shape = jax.ShapeDtypeStruct((m // seq_len, A_HEADS, 2, seq_len, HEAD_W), BF16)
        q_spec = pl.BlockSpec((None, A_HEADS, 2, tm, HEAD_W),
                              lambda i: (i // tps, 0, 0, i % tps, 0))
    else:
        tab_map = lambda i: (0, 0)
        q_shape = jax.ShapeDtypeStruct((m, BRANCH_W), BF16)
        q_spec = pl.BlockSpec((tm, BRANCH_W), lambda i: (i, 0))
    col = lambda c: pl.BlockSpec((tm, BRANCH_W), lambda i, c=c: (i, c))
    row = pl.BlockSpec((tm, BRANCH_W), lambda i: (i, 0))
    return pl.pallas_call(
        functools.partial(_rope_kernel, stacked_q=stacked),
        out_shape=(jax.ShapeDtypeStruct((m, BRANCH_W), F32),
                   jax.ShapeDtypeStruct((m, BRANCH_W), F32),
                   jax.ShapeDtypeStruct((m, BRANCH_W), BF16),
                   jax.ShapeDtypeStruct((m, BRANCH_W), BF16),
                   q_shape),
        grid=(m // tm,),
        in_specs=[col(COL_Q), col(COL_K), col(COL_V),
                  pl.BlockSpec((tm, HEAD_W), tab_map),
                  pl.BlockSpec((tm, HEAD_W), tab_map)],
        out_specs=(row, row, row, row, q_spec),
        compiler_params=_params("parallel"),
        name="rope",
    )(proj, proj, proj, cos, sin_signed)


def _rope_tables(pos):
    half = A_HD // 2
    inv = 1.0 / (ROPE_THETA ** (jnp.arange(half, dtype=F32) / half))
    ang = pos.astype(F32)[:, None] * inv[None, :]
    cos, sin = jnp.cos(ang), jnp.sin(ang)
    cos = jnp.concatenate([cos, cos], axis=-1)
    sin = jnp.concatenate([-sin, sin], axis=-1)
    return jnp.tile(cos, (1, 2)), jnp.tile(sin, (1, 2))


def _lambda(lam_ref, lam_init):
    lv = lam_ref[...]
    l1 = jnp.sum(lv[0:1] * lv[1:2], axis=-1, keepdims=True)
    l2 = jnp.sum(lv[2:3] * lv[3:4], axis=-1, keepdims=True)
    return jnp.exp(l1) - jnp.exp(l2) + lam_init


def _diff_head_out(o1, o2, lam, subg, z, lam_init):
    d = o1 - lam * o2
    y = d * lax.rsqrt(jnp.mean(d * d, axis=-1, keepdims=True) + SUBLN_EPS) * subg
    return (y * (1.0 - lam_init)) * _silu(z)


_NT = (((1,), (1,)), ((), ()))
LANES = 128


def _lane_block_sum(p):
    out = p[:, :LANES]
    for c in range(1, p.shape[1] // LANES):
        out = out + p[:, c * LANES:(c + 1) * LANES]
    return out


def _softmax_block(s, m_prev):
    m_new = jnp.maximum(m_prev, jnp.max(s, axis=-1, keepdims=True))
    alpha = jnp.exp2(m_prev - m_new)
    p = jnp.exp2(s - jnp.tile(m_new, (1, s.shape[1] // LANES)))
    return m_new, alpha, p


def _attn_p_kernel(q_ref, k_ref, v_ref, z_ref, subg_ref, lam_ref, o_ref,
                   m_ref, l_ref, acc_ref, *, tq, tk, hpb, lam_init):
    qi = pl.program_id(2)
    rows = 2 * tq
    m_ref[...] = jnp.full(m_ref.shape, NEG, F32)
    l_ref[...] = jnp.zeros(l_ref.shape, F32)
    acc_ref[...] = jnp.zeros(acc_ref.shape, F32)

    def block(start, masked):
        for j in range(hpb):
            cols = slice(j * HEAD_W, (j + 1) * HEAD_W)
            q = q_ref[j].reshape(rows, HEAD_W)
            s = lax.dot_general(q, k_ref[pl.ds(start, tk), cols], _NT,
                                preferred_element_type=F32)
            if masked:
                r = lax.broadcasted_iota(jnp.int32, s.shape, 0)
                c = lax.broadcasted_iota(jnp.int32, s.shape, 1)
                s = jnp.where(start + c <= qi * tq + r % tq, s, NEG)
            m_new, alpha, p = _softmax_block(s, m_ref[j])
            l_ref[j] = alpha * l_ref[j] + _lane_block_sum(p)
            acc_ref[j] = alpha * acc_ref[j] + jnp.dot(
                p.astype(BF16), v_ref[pl.ds(start, tk), cols], preferred_element_type=F32)
            m_ref[j] = m_new

    n_full = (qi * tq) // tk

    def body(ki, carry):
        block(pl.multiple_of(ki * tk, tk), False)
        return carry

    lax.fori_loop(0, n_full, body, 0)
    block(pl.multiple_of(n_full * tk, tk), True)

    lam = _lambda(lam_ref, lam_init)
    for j in range(hpb):
        cols = slice(j * HEAD_W, (j + 1) * HEAD_W)
        o = acc_ref[j] / jnp.sum(l_ref[j], axis=-1, keepdims=True)
        out = _diff_head_out(o[:tq], o[tq:], lam, subg_ref[...], z_ref[:, cols], lam_init)
        o_ref[:, cols] = out.astype(o_ref.dtype)


def _attn_prompt(q_st, k_bf, v_bf, proj, subg, lam_vecs, lam_init, *, batch, seq_len,
                 tq=256, tk=256, hpb=8):
    nq = seq_len // tq
    gw = hpb * HEAD_W
    zcol = COL_Z * (BRANCH_W // gw)
    kv_spec = pl.BlockSpec((seq_len, gw), lambda b, g, qi: (b, g))
    return pl.pallas_call(
        functools.partial(_attn_p_kernel, tq=tq, tk=tk, hpb=hpb, lam_init=lam_init),
        out_shape=jax.ShapeDtypeStruct((batch * seq_len, BRANCH_W), BF16),
        grid=(batch, A_HEADS // hpb, nq),
        in_specs=[pl.BlockSpec((None, hpb, 2, tq, HEAD_W), lambda b, g, qi: (b, g, 0, qi, 0)),
                  kv_spec, kv_spec,
                  pl.BlockSpec((tq, gw), lambda b, g, qi: (b * nq + qi, zcol + g)),
                  pl.BlockSpec((1, HEAD_W), lambda b, g, qi: (0, 0)),
                  pl.BlockSpec((4, A_HD), lambda b, g, qi: (0, 0))],
        out_specs=pl.BlockSpec((tq, gw), lambda b, g, qi: (b * nq + qi, g)),
        scratch_shapes=[pltpu.VMEM((hpb, 2 * tq, LANES), F32),
                        pltpu.VMEM((hpb, 2 * tq, LANES), F32),
                        pltpu.VMEM((hpb, 2 * tq, HEAD_W), F32)],
        compiler_params=_params("parallel", "parallel", "arbitrary"),
        name="attn_prompt",
    )(q_st, k_bf, v_bf, proj, subg, lam_vecs)


def _attn_s_kernel(pt_ref, q_ref, kn_ref, vn_ref, *rest, t, pps, lam_init):
    del pt_ref
    kc_refs, vc_refs = rest[:pps], rest[pps:2 * pps]
    (z_ref, subg_ref, lam_ref, o_ref,
     qbd_ref, knp_ref, vnp_ref, m_ref, l_ref, acc_ref) = rest[2 * pps:]
    p_idx = pl.program_id(1)
    hrows = 2 * t

    def stats(s):
        m_new, alpha, p = _softmax_block(s, m_ref[...])
        l_ref[...] = alpha * l_ref[...] + _lane_block_sum(p)
        m_ref[...] = m_new
        return alpha, p.astype(BF16)

    @pl.when((p_idx == 0) & (pl.program_id(0) == 0))
    def _():
        knp_ref[...] = jnp.zeros(knp_ref.shape, BF16)
        vnp_ref[...] = jnp.zeros(vnp_ref.shape, BF16)

    @pl.when(p_idx == 0)
    def _():
        q = q_ref[...]
        lane = lax.broadcasted_iota(jnp.int32, q.shape, 1)
        for hc in range(2 * A_HEADS):
            keep = (lane >= hc * A_HD) & (lane < (hc + 1) * A_HD)
            qbd_ref[hc * t:(hc + 1) * t, :] = jnp.where(keep, q, jnp.zeros_like(q))
        m_ref[...] = jnp.full(m_ref.shape, NEG, F32)
        l_ref[...] = jnp.zeros(l_ref.shape, F32)
        knp_ref[0:t, :] = kn_ref[...]
        vnp_ref[0:t, :] = vn_ref[...]
        s = lax.dot_general(qbd_ref[...], knp_ref[...], _NT, preferred_element_type=F32)
        r = lax.broadcasted_iota(jnp.int32, s.shape, 0)
        c = lax.broadcasted_iota(jnp.int32, s.shape, 1)
        s = jnp.where(c <= (r % t), s, NEG)
        _, pb = stats(s)
        acc_ref[...] = jnp.concatenate(
            [jnp.dot(pb[h * hrows:(h + 1) * hrows], vnp_ref[:, h * HEAD_W:(h + 1) * HEAD_W],
                     preferred_element_type=F32) for h in range(A_HEADS)], axis=0)

    kcat = jnp.concatenate(
        [r[...].reshape(BRANCH_W, PAGE_SIZE).astype(BF16) for r in kc_refs], axis=1)
    s = jnp.dot(qbd_ref[...], kcat, preferred_element_type=F32)
    alpha, pb = stats(s)
    parts = []
    for h in range(A_HEADS):
        vh = jnp.concatenate(
            [r[pl.ds(h, PAGE_SIZE, stride=A_HEADS), :].astype(BF16) for r in vc_refs], axis=0)
        parts.append(jnp.dot(pb[h * hrows:(h + 1) * hrows], vh, preferred_element_type=F32))
    acc_ref[...] = alpha * acc_ref[...] + jnp.concatenate(parts, axis=0)

    @pl.when(p_idx == pl.num_programs(1) - 1)
    def _():
        o = acc_ref[...] / jnp.sum(l_ref[...], axis=-1, keepdims=True)
        lam = _lambda(lam_ref, lam_init)
        subg = subg_ref[...]
        for h in range(A_HEADS):
            o1 = o[(2 * h) * t:(2 * h + 1) * t]
            o2 = o[(2 * h + 1) * t:(2 * h + 2) * t]
            zh = z_ref[:, h * HEAD_W:(h + 1) * HEAD_W]
            out = _diff_head_out(o1, o2, lam, subg, zh, lam_init)
            o_ref[:, h * HEAD_W:(h + 1) * HEAD_W] = out.astype(o_ref.dtype)


def _attn_sample(q_bf, k_bf, v_bf, cache_k, cache_v, layer, page_table, proj, subg, lam_vecs,
                 lam_init, *, n_seq, t, pps=16):
    n_pages = page_table.shape[1]
    rows = 2 * A_HEADS * t
    pt_flat = page_table.reshape(-1)
    kc = jnp.transpose(cache_k, (0, 1, 3, 4, 2))
    vc = cache_v.reshape(cache_v.shape[0], cache_v.shape[1], PAGE_SIZE * A_HEADS, HEAD_W)
    seq_spec = pl.BlockSpec((t, BRANCH_W), lambda b, p, pt: (b, 0))

    def page(i, ndim=3):
        return lambda b, p, pt: (layer, pt[b * n_pages + p * pps + i]) + (0,) * ndim

    k_specs = [pl.BlockSpec((None, None, 2 * A_HEADS, A_HD, PAGE_SIZE), page(i))
               for i in range(pps)]
    v_specs = [pl.BlockSpec((None, None, PAGE_SIZE * A_HEADS, HEAD_W), page(i, 2))
               for i in range(pps)]
    grid_spec = pltpu.PrefetchScalarGridSpec(
        num_scalar_prefetch=1,
        grid=(n_seq, n_pages // pps),
        in_specs=[seq_spec, seq_spec, seq_spec] + k_specs + v_specs + [
            pl.BlockSpec((t, BRANCH_W), lambda b, p, pt: (b, COL_Z)),
            pl.BlockSpec((1, HEAD_W), lambda b, p, pt: (0, 0)),
            pl.BlockSpec((4, A_HD), lambda b, p, pt: (0, 0))],
        out_specs=pl.BlockSpec((t, BRANCH_W), lambda b, p, pt: (b, 0)),
        scratch_shapes=[pltpu.VMEM((rows, BRANCH_W), BF16),
                        pltpu.VMEM((PAGE_SIZE, BRANCH_W), BF16),
                        pltpu.VMEM((PAGE_SIZE, BRANCH_W), BF16),
                        pltpu.VMEM((rows, LANES), F32), pltpu.VMEM((rows, LANES), F32),
                        pltpu.VMEM((rows, HEAD_W), F32)],
    )
    return pl.pallas_call(
        functools.partial(_attn_s_kernel, t=t, pps=pps, lam_init=lam_init),
        out_shape=jax.ShapeDtypeStruct((n_seq * t, BRANCH_W), BF16),
        grid_spec=grid_spec,
        compiler_params=_params("arbitrary", "arbitrary"),
        name="attn_sample",
    )(pt_flat, q_bf, k_bf, v_bf, *([kc] * pps), *([vc] * pps), proj, subg, lam_vecs)


def _window_sum(e, log2_win):
    s = e
    for k in range(log2_win):
        s = s + pltpu.roll(s, 1 << k, 0)
    return s


def _pool_groups(ext, pos, w_ref, scale_ref, z, o_ref, pick):
    for gi, win in enumerate(POOL_WINDOWS):
        c0, c1 = gi * POOL_GW, (gi + 1) * POOL_GW
        e = ext[:, c0:c1]
        cur = pick(e)
        tot = pick(_window_sum(e, gi + 1))
        cnt = jnp.minimum(pos + 1, win).astype(F32)
        mixed = tot / cnt - cur
        out = jnp.dot(mixed.astype(BF16), w_ref[gi].astype(BF16), preferred_element_type=F32)
        out = out * scale_ref[:, c0:c1]
        o_ref[:, c0:c1] = (out * _silu(z[:, c0:c1])).astype(o_ref.dtype)


def _pool_p_kernel(u_ref, prev_ref, z_ref, w_ref, scale_ref, o_ref, *, tm, tps):
    i = pl.program_id(0)
    first = (i % tps) == 0
    prev = jnp.where(first, 0.0, prev_ref[...])
    ext = jnp.concatenate([prev, u_ref[...]], axis=0)
    pos = (i % tps) * tm + lax.broadcasted_iota(jnp.int32, (tm, POOL_GW), 0)
    _pool_groups(ext, pos, w_ref, scale_ref, z_ref[...], o_ref, lambda x: x[POOL_PAD:])


def _pool_s_kernel(u_ref, buf_ref, z_ref, w_ref, scale_ref, o_ref, *, nb, t, past_len):
    u3 = u_ref[...].reshape(nb, t, BRANCH_W)
    ext = jnp.concatenate([buf_ref[...], u3], axis=1)
    ext = ext.reshape(nb * (POOL_PAD + t), BRANCH_W)
    pos = past_len + lax.broadcasted_iota(jnp.int32, (nb * t, POOL_GW), 0) % t

    def pick(x):
        x3 = x.reshape(nb, POOL_PAD + t, x.shape[-1])
        return x3[:, POOL_PAD:].reshape(nb * t, x.shape[-1])

    _pool_groups(ext, pos, w_ref, scale_ref, z_ref[...], o_ref, pick)


def _pool_prompt(proj, pool_w, pool_scale, layer, *, seq_len, tm=256):
    m = proj.shape[0]
    tps = seq_len // tm
    per = tm // POOL_PAD
    return pl.pallas_call(
        functools.partial(_pool_p_kernel, tm=tm, tps=tps),
        out_shape=jax.ShapeDtypeStruct((m, BRANCH_W), BF16),
        grid=(m // tm,),
        in_specs=[pl.BlockSpec((tm, BRANCH_W), lambda i: (i, COL_U)),
                  pl.BlockSpec((POOL_PAD, BRANCH_W),
                               lambda i: (jnp.maximum(i * per - 1, 0), COL_U)),
                  pl.BlockSpec((tm, BRANCH_W), lambda i: (i, COL_Z + 1)),
                  pl.BlockSpec((None,) + pool_w.shape[1:], lambda i: (layer, 0, 0, 0)),
                  pl.BlockSpec((1, BRANCH_W), lambda i: (0, 0))],
        out_specs=pl.BlockSpec((tm, BRANCH_W), lambda i: (i, 0)),
        compiler_params=_params("parallel"),
        name="pool_prompt",
    )(proj, proj, proj, pool_w, pool_scale.reshape(1, BRANCH_W))


def _pool_sample(proj, buf_padded, pool_w, pool_scale, layer, *, t, past_len, nb=32):
    m = proj.shape[0]
    return pl.pallas_call(
        functools.partial(_pool_s_kernel, nb=nb, t=t, past_len=past_len),
        out_shape=jax.ShapeDtypeStruct((m, BRANCH_W), BF16),
        grid=(m // (nb * t),),
        in_specs=[pl.BlockSpec((nb * t, BRANCH_W), lambda i: (i, COL_U)),
                  pl.BlockSpec((nb, POOL_PAD, BRANCH_W), lambda i: (i, 0, 0)),
                  pl.BlockSpec((nb * t, BRANCH_W), lambda i: (i, COL_Z + 1)),
                  pl.BlockSpec((None,) + pool_w.shape[1:], lambda i: (layer, 0, 0, 0)),
                  pl.BlockSpec((1, BRANCH_W), lambda i: (0, 0))],
        out_specs=pl.BlockSpec((nb * t, BRANCH_W), lambda i: (i, 0)),
        compiler_params=_params("parallel"),
        name="pool_sample",
    )(proj, buf_padded, proj, pool_w, pool_scale.reshape(1, BRANCH_W))


def _conv_taps(ext, pick, cw_ref, cb_ref):
    out = cb_ref[...] + pick(ext) * cw_ref[CONV_W - 1:CONV_W, :]
    for k in range(CONV_W - 1):
        shift = CONV_W - 1 - k
        out = out + pick(pltpu.roll(ext, shift, 0)) * cw_ref[k:k + 1, :]
    return out


def _block_diag_gate(xb, w_ref, b_ref):
    parts = [jnp.dot(xb[:, n * LRU_BW:(n + 1) * LRU_BW], w_ref[n].astype(BF16),
                     preferred_element_type=F32) for n in range(LRU_BLOCKS)]
    return jax.nn.sigmoid(jnp.concatenate(parts, axis=-1) + b_ref[...])


def _lru_coeffs(xc, wa_ref, ba_ref, wx_ref, bx_ref, lam_ref):
    xb = xc.astype(BF16)
    r = _block_diag_gate(xb, wa_ref, ba_ref)
    i = _block_diag_gate(xb, wx_ref, bx_ref)
    nl = -lam_ref[...]
    softplus = jnp.maximum(nl, 0.0) + jnp.log1p(jnp.exp(-jnp.abs(nl)))
    log_a = -LRU_C * r * softplus
    a = jnp.exp(log_a)
    b = jnp.sqrt(1.0 - a * a) * (i * xc)
    return a, b


def _group_scan(a, b):
    shape = a.shape
    grouped = (shape[0] // SUBLANES, SUBLANES, shape[1])
    a = a.reshape(grouped)
    b = b.reshape(grouped)
    row = lax.broadcasted_iota(jnp.int32, grouped, 1)
    d = 1
    while d < SUBLANES:
        valid = row >= d
        a_sh = pltpu.roll(a, d, 1)
        b_sh = pltpu.roll(b, d, 1)
        b = jnp.where(valid, a * b_sh + b, b)
        a = jnp.where(valid, a * a_sh, a)
        d *= 2
    return a.reshape(shape), b.reshape(shape)


def _lru_p_kernel(x_ref, prev_ref, z_ref, cw_ref, cb_ref, wa_ref, ba_ref, wx_ref, bx_ref,
                  lam_ref, o_ref, hl_ref, a_s, b_s, h_s, *, tm, tps):
    i = pl.program_id(0)
    first = (i % tps) == 0

    @pl.when(first)
    def _():
        h_s[...] = jnp.zeros(h_s.shape, F32)

    prev = jnp.where(first, 0.0, prev_ref[...])
    ext = jnp.concatenate([prev, x_ref[...]], axis=0)
    xc = _conv_taps(ext, lambda x: x[CONV_PAD:], cw_ref, cb_ref)
    a, b = _lru_coeffs(xc, wa_ref, ba_ref, wx_ref, bx_ref, lam_ref)
    a, b = _group_scan(a, b)
    a_s[...] = a
    b_s[...] = b

    def body(g, h):
        r0 = pl.multiple_of(g * SUBLANES, SUBLANES)
        hs = a_s[pl.ds(r0, SUBLANES), :] * h + b_s[pl.ds(r0, SUBLANES), :]
        b_s[pl.ds(r0, SUBLANES), :] = hs
        return hs[SUBLANES - 1:SUBLANES, :]

    h = lax.fori_loop(0, tm // SUBLANES, body, h_s[...])
    h_s[...] = h
    hl_ref[...] = h
    o_ref[...] = (b_s[...] * _silu(z_ref[...])).astype(o_ref.dtype)


def _lru_s_kernel(x_ref, buf_ref, h0_ref, z_ref, cw_ref, cb_ref, wa_ref, ba_ref, wx_ref, bx_ref,
                  lam_ref, o_ref, hs_ref, *, nb, t):
    x3 = x_ref[...].reshape(nb, t, BRANCH_W)
    ext = jnp.concatenate([buf_ref[...], x3], axis=1).reshape(nb * (CONV_PAD + t), BRANCH_W)

    def pick(x):
        return x.reshape(nb, CONV_PAD + t, BRANCH_W)[:, CONV_PAD:].reshape(nb * t, BRANCH_W)

    xc = _conv_taps(ext, pick, cw_ref, cb_ref)
    a, b = _lru_coeffs(xc, wa_ref, ba_ref, wx_ref, bx_ref, lam_ref)
    a, b = _group_scan(a, b)
    hs = a * h0_ref[...] + b
    hs_ref[...] = hs
    o_ref[...] = (hs * _silu(z_ref[...])).astype(o_ref.dtype)


def _lru_weight_specs(nargs_map):
    full = lambda shape: pl.BlockSpec(shape, lambda *a: (0,) * len(shape))
    del nargs_map
    return [full((CONV_W, BRANCH_W)), full((1, BRANCH_W)),
            full((LRU_BLOCKS, LRU_BW, LRU_BW)), full((1, BRANCH_W)),
            full((LRU_BLOCKS, LRU_BW, LRU_BW)), full((1, BRANCH_W)),
            full((1, BRANCH_W))]


def _lru_weights(conv_w, conv_b, wa, ba, wx, bx, lam):
    r = lambda v: v.reshape(1, BRANCH_W)
    return (conv_w, r(conv_b), wa, r(ba), wx, r(bx), r(lam))


def _lru_prompt(proj, weights, *, batch, seq_len, tm=256):
    m = proj.shape[0]
    tps = seq_len // tm
    per = tm // CONV_PAD
    return pl.pallas_call(
        functools.partial(_lru_p_kernel, tm=tm, tps=tps),
        out_shape=(jax.ShapeDtypeStruct((m, BRANCH_W), BF16),
                   jax.ShapeDtypeStruct((batch, 1, BRANCH_W), F32)),
        grid=(m // tm,),
        in_specs=[pl.BlockSpec((tm, BRANCH_W), lambda i: (i, COL_X)),
                  pl.BlockSpec((CONV_PAD, BRANCH_W),
                               lambda i: (jnp.maximum(i * per - 1, 0), COL_X)),
                  pl.BlockSpec((tm, BRANCH_W), lambda i: (i, COL_Z + 2))]
        + _lru_weight_specs(None),
        out_specs=(pl.BlockSpec((tm, BRANCH_W), lambda i: (i, 0)),
                   pl.BlockSpec((None, 1, BRANCH_W), lambda i: (i // tps, 0, 0))),
        scratch_shapes=[pltpu.VMEM((tm, BRANCH_W), F32), pltpu.VMEM((tm, BRANCH_W), F32),
                        pltpu.VMEM((1, BRANCH_W), F32)],
        compiler_params=_params("arbitrary"),
        name="lru_prompt",
    )(proj, proj, proj, *weights)


def _lru_sample(proj, buf_padded, h0_rows, weights, *, t, nb=32):
    m = proj.shape[0]
    rows = nb * t
    return pl.pallas_call(
        functools.partial(_lru_s_kernel, nb=nb, t=t),
        out_shape=(jax.ShapeDtypeStruct((m, BRANCH_W), BF16),
                   jax.ShapeDtypeStruct((m, BRANCH_W), F32)),
        grid=(m // rows,),
        in_specs=[pl.BlockSpec((rows, BRANCH_W), lambda i: (i, COL_X)),
                  pl.BlockSpec((nb, CONV_PAD, BRANCH_W), lambda i: (i, 0, 0)),
                  pl.BlockSpec((rows, BRANCH_W), lambda i: (i, 0)),
                  pl.BlockSpec((rows, BRANCH_W), lambda i: (i, COL_Z + 2))]
        + _lru_weight_specs(None),
        out_specs=(pl.BlockSpec((rows, BRANCH_W), lambda i: (i, 0)),
                   pl.BlockSpec((rows, BRANCH_W), lambda i: (i, 0))),
        compiler_params=_params("parallel"),
        name="lru_sample",
    )(proj, buf_padded, h0_rows, proj, *weights)


def _mem_attn_kernel(q_ref, k_ref, v_ref, z_ref, o_ref):
    q = q_ref[...].astype(BF16)
    k = k_ref[...].astype(BF16)
    v = v_ref[...].astype(BF16)
    z = z_ref[...]
    for h in range(M_HEADS):
        c0, c1 = h * M_HD, (h + 1) * M_HD
        s = lax.dot_general(q[:, c0:c1], k[:, c0:c1], _NT, preferred_element_type=F32)
        s = s * (M_HD ** -0.5)
        p = jnp.exp(s - jnp.max(s, axis=-1, keepdims=True))
        den = jnp.sum(p, axis=-1, keepdims=True)
        o = jnp.dot((p / den).astype(BF16), v[:, c0:c1], preferred_element_type=F32)
        o_ref[:, c0:c1] = (o * _silu(z[:, c0:c1])).astype(o_ref.dtype)


def _mem_attn(proj, mem_k, mem_v, layer, *, n_seq, rows_per_seq, tq):
    n_mem = mem_k.shape[1] // n_seq
    nq = rows_per_seq // tq
    kv_spec = pl.BlockSpec((None, n_mem, BRANCH_W), lambda b, qi: (layer, b, 0))
    return pl.pallas_call(
        _mem_attn_kernel,
        out_shape=jax.ShapeDtypeStruct((n_seq * rows_per_seq, BRANCH_W), BF16),
        grid=(n_seq, nq),
        in_specs=[pl.BlockSpec((tq, BRANCH_W), lambda b, qi: (b * nq + qi, COL_QD)),
                  kv_spec, kv_spec,
                  pl.BlockSpec((tq, BRANCH_W), lambda b, qi: (b * nq + qi, COL_Z + 3))],
        out_specs=pl.BlockSpec((tq, BRANCH_W), lambda b, qi: (b * nq + qi, 0)),
        compiler_params=_params("parallel", "parallel"),
        name="mem_attn",
    )(proj, mem_k, mem_v, proj)


def _mem_attn_s_kernel(q_ref, k_ref, v_ref, z_ref, o_ref, *, nb, t):
    n_mem = k_ref.shape[1]
    rows = M_HEADS * t
    col = lax.broadcasted_iota(jnp.int32, (rows, n_mem * M_HEADS), 1)
    row = lax.broadcasted_iota(jnp.int32, (rows, n_mem * M_HEADS), 0)
    own_head = (col % M_HEADS) == (row // t)
    for b in range(nb):
        q = q_ref[b * t:(b + 1) * t, :].astype(BF16)
        qs = jnp.concatenate([q[:, h * M_HD:(h + 1) * M_HD] for h in range(M_HEADS)], axis=0)
        k2 = k_ref[b].reshape(n_mem * M_HEADS, M_HD).astype(BF16)
        v2 = v_ref[b].reshape(n_mem * M_HEADS, M_HD).astype(BF16)
        s = lax.dot_general(qs, k2, _NT, preferred_element_type=F32) * (M_HD ** -0.5)
        s = jnp.where(own_head, s, NEG)
        p = jnp.exp(s - jnp.max(s, axis=-1, keepdims=True))
        den = jnp.sum(p, axis=-1, keepdims=True)
        o = jnp.dot((p / den).astype(BF16), v2, preferred_element_type=F32)
        for h in range(M_HEADS):
            c0, c1 = h * M_HD, (h + 1) * M_HD
            zh = z_ref[b * t:(b + 1) * t, c0:c1]
            o_ref[b * t:(b + 1) * t, c0:c1] = (o[h * t:(h + 1) * t] * _silu(zh)).astype(o_ref.dtype)


def _mem_attn_sample(proj, cache_mem_k, cache_mem_v, layer, *, t, nb=4):
    _, n_seq, n_mem, _, _ = cache_mem_k.shape
    kv_spec = pl.BlockSpec((None, nb, n_mem, M_HEADS, M_HD), lambda i: (layer, i, 0, 0, 0))
    return pl.pallas_call(
        functools.partial(_mem_attn_s_kernel, nb=nb, t=t),
        out_shape=jax.ShapeDtypeStruct((n_seq * t, BRANCH_W), BF16),
        grid=(n_seq // nb,),
        in_specs=[pl.BlockSpec((nb * t, BRANCH_W), lambda i: (i, COL_QD)),
                  kv_spec, kv_spec,
                  pl.BlockSpec((nb * t, BRANCH_W), lambda i: (i, COL_Z + 3))],
        out_specs=pl.BlockSpec((nb * t, BRANCH_W), lambda i: (i, 0)),
        compiler_params=_params("parallel"),
        name="mem_attn_sample",
    )(proj, cache_mem_k, cache_mem_v, proj)


def _merge_kernel(*refs):
    h_refs, g_refs = refs[:N_BRANCH], refs[N_BRANCH:2 * N_BRANCH]
    w_ref, o_ref = refs[2 * N_BRANCH:]
    acc = None
    for n in range(N_BRANCH):
        up = jnp.dot(h_refs[n][...], w_ref[n], preferred_element_type=F32)
        term = jax.nn.sigmoid(g_refs[n][...]) * up
        acc = term if acc is None else acc + term
    o_ref[...] = acc.astype(o_ref.dtype)


def _merge(h_a, h_b, h_c, h_d, proj, w_branch_bf, layer, tm=1024, tn=512):
    m = h_a.shape[0]
    tm = min(tm, m)
    h_spec = pl.BlockSpec((tm, BRANCH_W), lambda i, j: (i, 0))
    gate_col0 = COL_G * BRANCH_W // tn
    g_specs = [pl.BlockSpec((tm, tn), lambda i, j, n=n: (i, gate_col0 + n * (D_MODEL // tn) + j))
               for n in range(N_BRANCH)]
    return pl.pallas_call(
        _merge_kernel,
        out_shape=jax.ShapeDtypeStruct((m, D_MODEL), BF16),
        grid=(m // tm, D_MODEL // tn),
        in_specs=[h_spec] * N_BRANCH + g_specs + [
            pl.BlockSpec((None, N_BRANCH, BRANCH_W, tn), lambda i, j: (layer, 0, 0, j))],
        out_specs=pl.BlockSpec((tm, tn), lambda i, j: (i, j)),
        compiler_params=_params("parallel", "parallel"),
        name="merge",
    )(h_a, h_b, h_c, h_d, proj, proj, proj, proj, w_branch_bf)


def _layer_dense_in(x, ln_g, w_in, layer):
    xn = _rmsnorm(x, ln_g, EPS, BF16)
    return _matmul(xn, w_in, layer)


def _layer_dense_out(x, h_a, h_b, h_c, h_d, proj, w_branch_bf, w_out, layer):
    merged = _merge(h_a, h_b, h_c, h_d, proj, w_branch_bf, layer)
    return _matmul(merged, w_out, layer, res=x, tn=512)


def kernel(x_prompt, x_sample, cache_k, cache_v, cache_mem_k, cache_mem_v, state_pool, state_conv, state_h, page_table, mem_prompt, ln_g, w_in, lam_q1, lam_k1, lam_q2, lam_k2, attn_sub_g, pool_w, pool_scale, conv_w, conv_b, lru_wa, lru_ba, lru_wx, lru_bx, lru_lambda, mem_g, w_mem_k, w_mem_v, w_branch, w_out, final_g):
    batch, seq_len, _ = x_prompt.shape
    n_seq, t, _ = x_sample.shape
    depth = w_in.shape[0]
    n_mem = mem_prompt.shape[1]
    past_len = page_table.shape[1] * PAGE_SIZE
    rope_tm = 256

    cos_p, sin_p = _rope_tables(jnp.arange(seq_len, dtype=jnp.int32))
    cos_s, sin_s = _rope_tables(past_len + jnp.arange(t, dtype=jnp.int32))
    cos_s = jnp.tile(cos_s, (rope_tm // t, 1))
    sin_s = jnp.tile(sin_s, (rope_tm // t, 1))

    xp = x_prompt.reshape(batch * seq_len, D_MODEL)
    xs = x_sample.reshape(n_seq * t, D_MODEL)
    mem_rows = mem_prompt.reshape(batch * n_mem, D_MODEL)

    w_branch_bf = w_branch.astype(BF16)

    outs = {name: [] for name in ("kp", "vp", "ks", "vs", "mk", "mv", "pp", "ps", "cp", "cs", "hp", "hs")}
    for l in range(depth):
        lam_init = 0.8 - 0.6 * math.exp(-0.3 * l)
        lam_vecs = jnp.stack([lam_q1[l], lam_k1[l], lam_q2[l], lam_k2[l]])
        subg = attn_sub_g[l].reshape(1, HEAD_W)
        lru_w = _lru_weights(conv_w[l], conv_b[l], lru_wa[l], lru_ba[l], lru_wx[l], lru_bx[l],
                             lru_lambda[l])

        mn = _rmsnorm(mem_rows, mem_g[l], EPS, BF16)
        mk = _matmul(mn, w_mem_k, l)
        mv = _matmul(mn, w_mem_v, l)
        proj = _layer_dense_in(xp, ln_g[l], w_in, l)
        k_f, v_f, k_bf, v_bf, q_st = _rope_call(proj, cos_p, sin_p, seq_len=seq_len, tm=rope_tm)
        h_a = _attn_prompt(q_st, k_bf, v_bf, proj, subg, lam_vecs, lam_init,
                           batch=batch, seq_len=seq_len)
        h_b = _pool_prompt(proj, pool_w, pool_scale[l], l, seq_len=seq_len)
        h_c, h_last = _lru_prompt(proj, lru_w, batch=batch, seq_len=seq_len)
        h_d = _mem_attn(proj, mk[None], mv[None], 0, n_seq=batch, rows_per_seq=seq_len, tq=512)
        proj3 = proj.reshape(batch, seq_len, -1)
        outs["pp"].append(proj3[:, seq_len - POOL_BUF:, COL_U * BRANCH_W:(COL_U + 1) * BRANCH_W])
        outs["cp"].append(proj3[:, seq_len - (CONV_W - 1):, COL_X * BRANCH_W:(COL_X + 1) * BRANCH_W])
        xp = _layer_dense_out(xp, h_a, h_b, h_c, h_d, proj, w_branch_bf, w_out, l)
        outs["kp"].append(k_f.reshape(batch, seq_len, 2 * A_HEADS, A_HD))
        outs["vp"].append(v_f.reshape(batch, seq_len, A_HEADS, HEAD_W))
        outs["mk"].append(mk.reshape(batch, n_mem, M_HEADS, M_HD))
        outs["mv"].append(mv.reshape(batch, n_mem, M_HEADS, M_HD))
        outs["hp"].append(h_last.reshape(batch, BRANCH_W))

        proj = _layer_dense_in(xs, ln_g[l], w_in, l)
        k_f, v_f, k_bf, v_bf, q_bf = _rope_call(proj, cos_s, sin_s, seq_len=0, tm=rope_tm)
        h_a = _attn_sample(q_bf, k_bf, v_bf, cache_k, cache_v, l, page_table, proj, subg,
                           lam_vecs, lam_init, n_seq=n_seq, t=t)
        pool_buf = jnp.pad(state_pool[l], ((0, 0), (POOL_PAD - POOL_BUF, 0), (0, 0)))
        h_b = _pool_sample(proj, pool_buf, pool_w, pool_scale[l], l, t=t, past_len=past_len)
        conv_buf = jnp.pad(state_conv[l], ((0, 0), (CONV_PAD - (CONV_W - 1), 0), (0, 0)))
        h0_rows = jnp.repeat(state_h[l], t, axis=0)
        h_c, hs_rows = _lru_sample(proj, conv_buf, h0_rows, lru_w, t=t)
        h_d = _mem_attn_sample(proj, cache_mem_k, cache_mem_v, l, t=t)
        proj3 = proj.reshape(n_seq, t, -1)
        u_s = proj3[:, :, COL_U * BRANCH_W:(COL_U + 1) * BRANCH_W]
        x_s = proj3[:, :, COL_X * BRANCH_W:(COL_X + 1) * BRANCH_W]
        outs["ps"].append(jnp.concatenate([state_pool[l], u_s], axis=1)[:, -POOL_BUF:])
        outs["cs"].append(jnp.concatenate([state_conv[l], x_s], axis=1)[:, -(CONV_W - 1):])
        xs = _layer_dense_out(xs, h_a, h_b, h_c, h_d, proj, w_branch_bf, w_out, l)
        outs["ks"].append(k_f.reshape(n_seq, t, 2 * A_HEADS, A_HD))
        outs["vs"].append(v_f.reshape(n_seq, t, A_HEADS, HEAD_W))
        outs["hs"].append(hs_rows.reshape(n_seq, t, BRANCH_W)[:, t - 1])

    y_prompt = _rmsnorm(xp, final_g, EPS, F32).reshape(batch, seq_len, D_MODEL)
    y_sample = _rmsnorm(xs, final_g, EPS, F32).reshape(n_seq, t, D_MODEL)
    st = lambda name: jnp.stack(outs[name])
    return (y_prompt, y_sample, st("kp"), st("vp"), st("ks"), st("vs"), st("mk"), st("mv"),
            st("pp"), st("ps"), st("cp"), st("cs"), st("hp"), st("hs"))
```

```python
import functools
import math

import jax
import jax.numpy as jnp
from jax import lax
from jax.experimental import pallas as pl
from jax.experimental.pallas import tpu as pltpu

F32 = jnp.float32
BF16 = jnp.bfloat16

D_MODEL = 2048
BRANCH_W = 1024
A_HEADS = 8
A_HD = 64
HEAD_W = 2 * A_HD
ROPE_THETA = 10000.0
SUBLN_EPS = 1e-5
POOL_WINDOWS = (2, 4, 8, 16)
POOL_GW = BRANCH_W // len(POOL_WINDOWS)
POOL_BUF = max(POOL_WINDOWS) - 1
POOL_PAD = 16
LRU_BLOCKS = 8
LRU_BW = BRANCH_W // LRU_BLOCKS
CONV_W = 4
CONV_PAD = 8
LRU_C = 8.0
M_HEADS = 4
M_HD = BRANCH_W // M_HEADS
N_BRANCH = 4
PAGE_SIZE = 128
EPS = 1e-6
NEG = -1e30
SUBLANES = 8
LOG2_E = math.log2(math.e)

COL_Q, COL_K, COL_V, COL_U, COL_X, COL_QD = 0, 1, 2, 3, 4, 5
COL_Z = 6
COL_G = 10

VMEM_LIMIT = 56 * 1024 * 1024


def _params(*sem):
    return pltpu.CompilerParams(dimension_semantics=sem, vmem_limit_bytes=VMEM_LIMIT)


def _silu(z):
    return z * jax.nn.sigmoid(z)


def _rmsnorm_kernel(x_ref, g_ref, o_ref, *, eps):
    x = x_ref[...]
    y = x * lax.rsqrt(jnp.mean(x * x, axis=-1, keepdims=True) + eps)
    o_ref[...] = (y * g_ref[...]).astype(o_ref.dtype)


def _rmsnorm(x, g, eps, out_dtype, tm=512):
    m, d = x.shape
    return pl.pallas_call(
        functools.partial(_rmsnorm_kernel, eps=eps),
        out_shape=jax.ShapeDtypeStruct((m, d), out_dtype),
        grid=(m // tm,),
        in_specs=[pl.BlockSpec((tm, d), lambda i: (i, 0)),
                  pl.BlockSpec((1, d), lambda i: (0, 0))],
        out_specs=pl.BlockSpec((tm, d), lambda i: (i, 0)),
        compiler_params=_params("parallel"),
        name="rmsnorm",
    )(x, g.reshape(1, d))


def _mm_kernel(a_ref, w_ref, *rest, has_res):
    if has_res:
        r_ref, o_ref, wb_ref = rest
    else:
        o_ref, wb_ref = rest

    @pl.when(pl.program_id(1) == 0)
    def _():
        wb_ref[...] = w_ref[...].astype(BF16)

    acc = jnp.dot(a_ref[...], wb_ref[...], preferred_element_type=F32)
    if has_res:
        acc = r_ref[...] + acc
    o_ref[...] = acc


def _matmul(a, w, layer, res=None, tm=1024, tn=1024):
    m, k = a.shape
    n = w.shape[2]
    tm = min(tm, m)
    in_specs = [pl.BlockSpec((tm, k), lambda j, i: (i, 0)),
                pl.BlockSpec((None, k, tn), lambda j, i: (layer, 0, j))]
    args = [a, w]
    if res is not None:
        in_specs.append(pl.BlockSpec((tm, tn), lambda j, i: (i, j)))
        args.append(res)
    return pl.pallas_call(
        functools.partial(_mm_kernel, has_res=res is not None),
        out_shape=jax.ShapeDtypeStruct((m, n), F32),
        grid=(n // tn, m // tm),
        in_specs=in_specs,
        out_specs=pl.BlockSpec((tm, tn), lambda j, i: (i, j)),
        scratch_shapes=[pltpu.VMEM((k, tn), BF16)],
        compiler_params=_params("arbitrary", "arbitrary"),
        name="matmul",
    )(*args)


def _rope(x, cos, sin_signed):
    lane = lax.broadcasted_iota(jnp.int32, x.shape, 1)
    first_half = (lane % A_HD) < (A_HD // 2)
    width = x.shape[1]
    partner = jnp.where(first_half,
                        pltpu.roll(x, width - A_HD // 2, 1),
                        pltpu.roll(x, A_HD // 2, 1))
    reps = width // HEAD_W
    return x * jnp.tile(cos, (1, reps)) + partner * jnp.tile(sin_signed, (1, reps))


def _rope_kernel(q_ref, k_ref, v_ref, cos_ref, sin_ref,
                 ko_ref, vo_ref, kb_ref, vb_ref, qo_ref, *, stacked_q):
    cos = cos_ref[...]
    sin = sin_ref[...]
    k = _rope(k_ref[...], cos, sin)
    ko_ref[...] = k.T if stacked_q else k
    kb_ref[...] = k.astype(BF16)
    v = v_ref[...]
    vo_ref[...] = v
    vb_ref[...] = v.astype(BF16)
    q = _rope(q_ref[...], cos, sin) * (A_HD ** -0.5 * LOG2_E)
    if stacked_q:
        tm = q.shape[0]
        lane = lax.broadcasted_iota(jnp.int32, (tm, HEAD_W), 1)
        for h in range(A_HEADS):
            qh = q[:, h * HEAD_W:(h + 1) * HEAD_W]
            qo_ref[h, 0] = jnp.where(lane < A_HD, qh, 0.0).astype(BF16)
            qo_ref[h, 1] = jnp.where(lane >= A_HD, qh, 0.0).astype(BF16)
    else:
        qo_ref[...] = q.astype(BF16)


def _rope_call(proj, cos, sin_signed, *, seq_len, tm):
    m = proj.shape[0]
    stacked = seq_len > 0
    if stacked:
        tps = seq_len // tm
        tab_map = lambda i: (i % tps, 0)
        q_shape = jax.ShapeDtypeStruct((m // seq_len, A_HEADS, 2, seq_len, HEAD_W), BF16)
        q_spec = pl.BlockSpec((None, A_HEADS, 2, tm, HEAD_W),
                              lambda i: (i // tps, 0, 0, i % tps, 0))
        k_shape = jax.ShapeDtypeStruct((m // seq_len, BRANCH_W, seq_len), F32)
        k_spec = pl.BlockSpec((None, BRANCH_W, tm), lambda i: (i // tps, 0, i % tps))
    else:
        tab_map = lambda i: (0, 0)
        q_shape = jax.ShapeDtypeStruct((m, BRANCH_W), BF16)
        q_spec = pl.BlockSpec((tm, BRANCH_W), lambda i: (i, 0))
        k_shape = jax.ShapeDtypeStruct((m, BRANCH_W), F32)
        k_spec = pl.BlockSpec((tm, BRANCH_W), lambda i: (i, 0))
    col = lambda c: pl.BlockSpec((tm, BRANCH_W), lambda i, c=c: (i, c))
    row = pl.BlockSpec((tm, BRANCH_W), lambda i: (i, 0))
    return pl.pallas_call(
        functools.partial(_rope_kernel, stacked_q=stacked),
        out_shape=(k_shape,
                   jax.ShapeDtypeStruct((m, BRANCH_W), F32),
                   jax.ShapeDtypeStruct((m, BRANCH_W), BF16),
                   jax.ShapeDtypeStruct((m, BRANCH_W), BF16),
                   q_shape),
        grid=(m // tm,),
        in_specs=[col(COL_Q), col(COL_K), col(COL_V),
                  pl.BlockSpec((tm, HEAD_W), tab_map),
                  pl.BlockSpec((tm, HEAD_W), tab_map)],
        out_specs=(k_spec, row, row, row, q_spec),
        compiler_params=_params("parallel"),
        name="rope",
    )(proj, proj, proj, cos, sin_signed)


def _rope_tables(pos):
    half = A_HD // 2
    inv = 1.0 / (ROPE_THETA ** (jnp.arange(half, dtype=F32) / half))
    ang = pos.astype(F32)[:, None] * inv[None, :]
    cos, sin = jnp.cos(ang), jnp.sin(ang)
    cos = jnp.concatenate([cos, cos], axis=-1)
    sin = jnp.concatenate([-sin, sin], axis=-1)
    return jnp.tile(cos, (1, 2)), jnp.tile(sin, (1, 2))


def _lambda(lam_ref, lam_init):
    lv = lam_ref[...]
    l1 = jnp.sum(lv[0:1] * lv[1:2], axis=-1, keepdims=True)
    l2 = jnp.sum(lv[2:3] * lv[3:4], axis=-1, keepdims=True)
    return jnp.exp(l1) - jnp.exp(l2) + lam_init


def _diff_head_out(o1, o2, lam, subg, z, lam_init):
    d = o1 - lam * o2
    y = d * lax.rsqrt(jnp.mean(d * d, axis=-1, keepdims=True) + SUBLN_EPS) * subg
    return (y * (1.0 - lam_init)) * _silu(z)


_NT = (((1,), (1,)), ((), ()))
LANES = 128


def _lane_block_sum(p):
    out = p[:, :LANES]
    for c in range(1, p.shape[1] // LANES):
        out = out + p[:, c * LANES:(c + 1) * LANES]
    return out


def _softmax_block(s, m_prev):
    m_new = jnp.maximum(m_prev, jnp.max(s, axis=-1, keepdims=True))
    alpha = jnp.exp2(m_prev - m_new)
    p = jnp.exp2(s - jnp.tile(m_new, (1, s.shape[1] // LANES)))
    return m_new, alpha, p


def _attn_p_kernel(q_ref, k_ref, v_ref, z_ref, subg_ref, lam_ref, o_ref,
                   m_ref, l_ref, acc_ref, *, tq, tk, hpb, lam_init):
    qi = pl.program_id(2)
    rows = 2 * tq
    m_ref[...] = jnp.full(m_ref.shape, NEG, F32)
    l_ref[...] = jnp.zeros(l_ref.shape, F32)
    acc_ref[...] = jnp.zeros(acc_ref.shape, F32)

    def block(start, masked):
        for j in range(hpb):
            cols = slice(j * HEAD_W, (j + 1) * HEAD_W)
            q = q_ref[j].reshape(rows, HEAD_W)
            s = lax.dot_general(q, k_ref[pl.ds(start, tk), cols], _NT,
                                preferred_element_type=F32)
            if masked:
                r = lax.broadcasted_iota(jnp.int32, s.shape, 0)
                c = lax.broadcasted_iota(jnp.int32, s.shape, 1)
                s = jnp.where(start + c <= qi * tq + r % tq, s, NEG)
            m_new, alpha, p = _softmax_block(s, m_ref[j])
            l_ref[j] = alpha * l_ref[j] + _lane_block_sum(p)
            acc_ref[j] = alpha * acc_ref[j] + jnp.dot(
                p.astype(BF16), v_ref[pl.ds(start, tk), cols], preferred_element_type=F32)
            m_ref[j] = m_new

    n_full = (qi * tq) // tk

    def body(ki, carry):
        block(pl.multiple_of(ki * tk, tk), False)
        return carry

    lax.fori_loop(0, n_full, body, 0)
    block(pl.multiple_of(n_full * tk, tk), True)

    lam = _lambda(lam_ref, lam_init)
    for j in range(hpb):
        cols = slice(j * HEAD_W, (j + 1) * HEAD_W)
        o = acc_ref[j] / jnp.sum(l_ref[j], axis=-1, keepdims=True)
        out = _diff_head_out(o[:tq], o[tq:], lam, subg_ref[...], z_ref[:, cols], lam_init)
        o_ref[:, cols] = out.astype(o_ref.dtype)


def _attn_prompt(q_st, k_bf, v_bf, proj, subg, lam_vecs, lam_init, *, batch, seq_len,
                 tq=256, tk=256, hpb=8):
    nq = seq_len // tq
    gw = hpb * HEAD_W
    zcol = COL_Z * (BRANCH_W // gw)
    kv_spec = pl.BlockSpec((seq_len, gw), lambda b, g, qi: (b, g))
    return pl.pallas_call(
        functools.partial(_attn_p_kernel, tq=tq, tk=tk, hpb=hpb, lam_init=lam_init),
        out_shape=jax.ShapeDtypeStruct((batch * seq_len, BRANCH_W), BF16),
        grid=(batch, A_HEADS // hpb, nq),
        in_specs=[pl.BlockSpec((None, hpb, 2, tq, HEAD_W), lambda b, g, qi: (b, g, 0, qi, 0)),
                  kv_spec, kv_spec,
                  pl.BlockSpec((tq, gw), lambda b, g, qi: (b * nq + qi, zcol + g)),
                  pl.BlockSpec((1, HEAD_W), lambda b, g, qi: (0, 0)),
                  pl.BlockSpec((4, A_HD), lambda b, g, qi: (0, 0))],
        out_specs=pl.BlockSpec((tq, gw), lambda b, g, qi: (b * nq + qi, g)),
        scratch_shapes=[pltpu.VMEM((hpb, 2 * tq, LANES), F32),
                        pltpu.VMEM((hpb, 2 * tq, LANES), F32),
                        pltpu.VMEM((hpb, 2 * tq, HEAD_W), F32)],
        compiler_params=_params("parallel", "parallel", "arbitrary"),
        name="attn_prompt",
    )(q_st, k_bf, v_bf, proj, subg, lam_vecs)


def _attn_s_kernel(pt_ref, q_ref, kn_ref, vn_ref, *rest, t, pps, lam_init):
    del pt_ref
    kc_refs, vc_refs = rest[:pps], rest[pps:2 * pps]
    (z_ref, subg_ref, lam_ref, o_ref,
     qbd_ref, knp_ref, vnp_ref, m_ref, l_ref, acc_ref) = rest[2 * pps:]
    p_idx = pl.program_id(1)
    hrows = 2 * t

    def stats(s):
        m_new, alpha, p = _softmax_block(s, m_ref[...])
        l_ref[...] = alpha * l_ref[...] + _lane_block_sum(p)
        m_ref[...] = m_new
        return alpha, p.astype(BF16)

    @pl.when((p_idx == 0) & (pl.program_id(0) == 0))
    def _():
        knp_ref[...] = jnp.zeros(knp_ref.shape, BF16)
        vnp_ref[...] = jnp.zeros(vnp_ref.shape, BF16)

    @pl.when(p_idx == 0)
    def _():
        q = q_ref[...]
        lane = lax.broadcasted_iota(jnp.int32, q.shape, 1)
        for hc in range(2 * A_HEADS):
            keep = (lane >= hc * A_HD) & (lane < (hc + 1) * A_HD)
            qbd_ref[hc * t:(hc + 1) * t, :] = jnp.where(keep, q, jnp.zeros_like(q))
        m_ref[...] = jnp.full(m_ref.shape, NEG, F32)
        l_ref[...] = jnp.zeros(l_ref.shape, F32)
        knp_ref[0:t, :] = kn_ref[...]
        vnp_ref[0:t, :] = vn_ref[...]
        s = lax.dot_general(qbd_ref[...], knp_ref[...], _NT, preferred_element_type=F32)
        r = lax.broadcasted_iota(jnp.int32, s.shape, 0)
        c = lax.broadcasted_iota(jnp.int32, s.shape, 1)
        s = jnp.where(c <= (r % t), s, NEG)
        _, pb = stats(s)
        acc_ref[...] = jnp.concatenate(
            [jnp.dot(pb[h * hrows:(h + 1) * hrows], vnp_ref[:, h * HEAD_W:(h + 1) * HEAD_W],
                     preferred_element_type=F32) for h in range(A_HEADS)], axis=0)

    kcat = jnp.concatenate(
        [r[...].reshape(BRANCH_W, PAGE_SIZE).astype(BF16) for r in kc_refs], axis=1)
    s = jnp.dot(qbd_ref[...], kcat, preferred_element_type=F32)
    alpha, pb = stats(s)
    parts = []
    for h in range(A_HEADS):
        vh = jnp.concatenate(
            [r[pl.ds(h, PAGE_SIZE, stride=A_HEADS), :].astype(BF16) for r in vc_refs], axis=0)
        parts.append(jnp.dot(pb[h * hrows:(h + 1) * hrows], vh, preferred_element_type=F32))
    acc_ref[...] = alpha * acc_ref[...] + jnp.concatenate(parts, axis=0)

    @pl.when(p_idx == pl.num_programs(1) - 1)
    def _():
        o = acc_ref[...] / jnp.sum(l_ref[...], axis=-1, keepdims=True)
        lam = _lambda(lam_ref, lam_init)
        subg = subg_ref[...]
        for h in range(A_HEADS):
            o1 = o[(2 * h) * t:(2 * h + 1) * t]
            o2 = o[(2 * h + 1) * t:(2 * h + 2) * t]
            zh = z_ref[:, h * HEAD_W:(h + 1) * HEAD_W]
            out = _diff_head_out(o1, o2, lam, subg, zh, lam_init)
            o_ref[:, h * HEAD_W:(h + 1) * HEAD_W] = out.astype(o_ref.dtype)


def _attn_sample(q_bf, k_bf, v_bf, cache_k, cache_v, layer, page_table, proj, subg, lam_vecs,
                 lam_init, *, n_seq, t, pps=16):
    n_pages = page_table.shape[1]
    rows = 2 * A_HEADS * t
    pt_flat = page_table.reshape(-1)
    kc = jnp.transpose(cache_k, (0, 1, 3, 4, 2))
    vc = cache_v.reshape(cache_v.shape[0], cache_v.shape[1], PAGE_SIZE * A_HEADS, HEAD_W)
    seq_spec = pl.BlockSpec((t, BRANCH_W), lambda b, p, pt: (b, 0))

    def page(i, ndim=3):
        return lambda b, p, pt: (layer, pt[b * n_pages + p * pps + i]) + (0,) * ndim

    k_specs = [pl.BlockSpec((None, None, 2 * A_HEADS, A_HD, PAGE_SIZE), page(i))
               for i in range(pps)]
    v_specs = [pl.BlockSpec((None, None, PAGE_SIZE * A_HEADS, HEAD_W), page(i, 2))
               for i in range(pps)]
    grid_spec = pltpu.PrefetchScalarGridSpec(
        num_scalar_prefetch=1,
        grid=(n_seq, n_pages // pps),
        in_specs=[seq_spec, seq_spec, seq_spec] + k_specs + v_specs + [
            pl.BlockSpec((t, BRANCH_W), lambda b, p, pt: (b, COL_Z)),
            pl.BlockSpec((1, HEAD_W), lambda b, p, pt: (0, 0)),
            pl.BlockSpec((4, A_HD), lambda b, p, pt: (0, 0))],
        out_specs=pl.BlockSpec((t, BRANCH_W), lambda b, p, pt: (b, 0)),
        scratch_shapes=[pltpu.VMEM((rows, BRANCH_W), BF16),
                        pltpu.VMEM((PAGE_SIZE, BRANCH_W), BF16),
                        pltpu.VMEM((PAGE_SIZE, BRANCH_W), BF16),
                        pltpu.VMEM((rows, LANES), F32), pltpu.VMEM((rows, LANES), F32),
                        pltpu.VMEM((rows, HEAD_W), F32)],
    )
    return pl.pallas_call(
        functools.partial(_attn_s_kernel, t=t, pps=pps, lam_init=lam_init),
        out_shape=jax.ShapeDtypeStruct((n_seq * t, BRANCH_W), BF16),
        grid_spec=grid_spec,
        compiler_params=_params("arbitrary", "arbitrary"),
        name="attn_sample",
    )(pt_flat, q_bf, k_bf, v_bf, *([kc] * pps), *([vc] * pps), proj, subg, lam_vecs)


def _window_sum(e, log2_win):
    s = e
    for k in range(log2_win):
        s = s + pltpu.roll(s, 1 << k, 0)
    return s


def _pool_groups(ext, pos, w_ref, scale_ref, z, o_ref, pick):
    for gi, win in enumerate(POOL_WINDOWS):
        c0, c1 = gi * POOL_GW, (gi + 1) * POOL_GW
        e = ext[:, c0:c1]
        cur = pick(e)
        tot = pick(_window_sum(e, gi + 1))
        cnt = jnp.minimum(pos + 1, win).astype(F32)
        mixed = tot / cnt - cur
        out = jnp.dot(mixed.astype(BF16), w_ref[gi].astype(BF16), preferred_element_type=F32)
        out = out * scale_ref[:, c0:c1]
        o_ref[:, c0:c1] = (out * _silu(z[:, c0:c1])).astype(o_ref.dtype)


def _pool_p_kernel(u_ref, prev_ref, z_ref, w_ref, scale_ref, o_ref, *, tm, tps):
    i = pl.program_id(0)
    first = (i % tps) == 0
    prev = jnp.where(first, 0.0, prev_ref[...])
    ext = jnp.concatenate([prev, u_ref[...]], axis=0)
    pos = (i % tps) * tm + lax.broadcasted_iota(jnp.int32, (tm, POOL_GW), 0)
    _pool_groups(ext, pos, w_ref, scale_ref, z_ref[...], o_ref, lambda x: x[POOL_PAD:])


def _pool_s_kernel(u_ref, buf_ref, z_ref, w_ref, scale_ref, o_ref, *, nb, t, past_len):
    u3 = u_ref[...].reshape(nb, t, BRANCH_W)
    ext = jnp.concatenate([buf_ref[...], u3], axis=1)
    ext = ext.reshape(nb * (POOL_PAD + t), BRANCH_W)
    pos = past_len + lax.broadcasted_iota(jnp.int32, (nb * t, POOL_GW), 0) % t

    def pick(x):
        x3 = x.reshape(nb, POOL_PAD + t, x.shape[-1])
        return x3[:, POOL_PAD:].reshape(nb * t, x.shape[-1])

    _pool_groups(ext, pos, w_ref, scale_ref, z_ref[...], o_ref, pick)


def _pool_prompt(proj, pool_w, pool_scale, layer, *, seq_len, tm=256):
    m = proj.shape[0]
    tps = seq_len // tm
    per = tm // POOL_PAD
    return pl.pallas_call(
        functools.partial(_pool_p_kernel, tm=tm, tps=tps),
        out_shape=jax.ShapeDtypeStruct((m, BRANCH_W), BF16),
        grid=(m // tm,),
        in_specs=[pl.BlockSpec((tm, BRANCH_W), lambda i: (i, COL_U)),
                  pl.BlockSpec((POOL_PAD, BRANCH_W),
                               lambda i: (jnp.maximum(i * per - 1, 0), COL_U)),
                  pl.BlockSpec((tm, BRANCH_W), lambda i: (i, COL_Z + 1)),
                  pl.BlockSpec((None,) + pool_w.shape[1:], lambda i: (layer, 0, 0, 0)),
                  pl.BlockSpec((1, BRANCH_W), lambda i: (0, 0))],
        out_specs=pl.BlockSpec((tm, BRANCH_W), lambda i: (i, 0)),
        compiler_params=_params("parallel"),
        name="pool_prompt",
    )(proj, proj, proj, pool_w, pool_scale.reshape(1, BRANCH_W))


def _pool_sample(proj, buf_padded, pool_w, pool_scale, layer, *, t, past_len, nb=32):
    m = proj.shape[0]
    return pl.pallas_call(
        functools.partial(_pool_s_kernel, nb=nb, t=t, past_len=past_len),
        out_shape=jax.ShapeDtypeStruct((m, BRANCH_W), BF16),
        grid=(m // (nb * t),),
        in_specs=[pl.BlockSpec((nb * t, BRANCH_W), lambda i: (i, COL_U)),
                  pl.BlockSpec((nb, POOL_PAD, BRANCH_W), lambda i: (i, 0, 0)),
                  pl.BlockSpec((nb * t, BRANCH_W), lambda i: (i, COL_Z + 1)),
                  pl.BlockSpec((None,) + pool_w.shape[1:], lambda i: (layer, 0, 0, 0)),
                  pl.BlockSpec((1, BRANCH_W), lambda i: (0, 0))],
        out_specs=pl.BlockSpec((nb * t, BRANCH_W), lambda i: (i, 0)),
        compiler_params=_params("parallel"),
        name="pool_sample",
    )(proj, buf_padded, proj, pool_w, pool_scale.reshape(1, BRANCH_W))


def _conv_taps(ext, pick, cw_ref, cb_ref):
    out = cb_ref[...] + pick(ext) * cw_ref[CONV_W - 1:CONV_W, :]
    for k in range(CONV_W - 1):
        shift = CONV_W - 1 - k
        out = out + pick(pltpu.roll(ext, shift, 0)) * cw_ref[k:k + 1, :]
    return out


def _block_diag_gate(xb, w_ref, b_ref):
    parts = [jnp.dot(xb[:, n * LRU_BW:(n + 1) * LRU_BW], w_ref[n].astype(BF16),
                     preferred_element_type=F32) for n in range(LRU_BLOCKS)]
    return jax.nn.sigmoid(jnp.concatenate(parts, axis=-1) + b_ref[...])


def _lru_coeffs(xc, wa_ref, ba_ref, wx_ref, bx_ref, lam_ref):
    xb = xc.astype(BF16)
    r = _block_diag_gate(xb, wa_ref, ba_ref)
    i = _block_diag_gate(xb, wx_ref, bx_ref)
    nl = -lam_ref[...]
    softplus = jnp.maximum(nl, 0.0) + jnp.log1p(jnp.exp(-jnp.abs(nl)))
    log_a = -LRU_C * r * softplus
    a = jnp.exp(log_a)
    b = jnp.sqrt(1.0 - a * a) * (i * xc)
    return a, b


def _group_scan(a, b):
    shape = a.shape
    grouped = (shape[0] // SUBLANES, SUBLANES, shape[1])
    a = a.reshape(grouped)
    b = b.reshape(grouped)
    row = lax.broadcasted_iota(jnp.int32, grouped, 1)
    d = 1
    while d < SUBLANES:
        valid = row >= d
        a_sh = pltpu.roll(a, d, 1)
        b_sh = pltpu.roll(b, d, 1)
        b = jnp.where(valid, a * b_sh + b, b)
        a = jnp.where(valid, a * a_sh, a)
        d *= 2
    return a.reshape(shape), b.reshape(shape)


def _lru_p_kernel(x_ref, prev_ref, z_ref, cw_ref, cb_ref, wa_ref, ba_ref, wx_ref, bx_ref,
                  lam_ref, o_ref, hl_ref, a_s, b_s, h_s, *, tm, tps):
    i = pl.program_id(0)
    first = (i % tps) == 0

    @pl.when(first)
    def _():
        h_s[...] = jnp.zeros(h_s.shape, F32)

    prev = jnp.where(first, 0.0, prev_ref[...])
    ext = jnp.concatenate([prev, x_ref[...]], axis=0)
    xc = _conv_taps(ext, lambda x: x[CONV_PAD:], cw_ref, cb_ref)
    a, b = _lru_coeffs(xc, wa_ref, ba_ref, wx_ref, bx_ref, lam_ref)
    a, b = _group_scan(a, b)
    a_s[...] = a
    b_s[...] = b

    def body(g, h):
        r0 = pl.multiple_of(g * SUBLANES, SUBLANES)
        hs = a_s[pl.ds(r0, SUBLANES), :] * h + b_s[pl.ds(r0, SUBLANES), :]
        b_s[pl.ds(r0, SUBLANES), :] = hs
        return hs[SUBLANES - 1:SUBLANES, :]

    h = lax.fori_loop(0, tm // SUBLANES, body, h_s[...])
    h_s[...] = h
    hl_ref[...] = h
    o_ref[...] = (b_s[...] * _silu(z_ref[...])).astype(o_ref.dtype)


def _lru_s_kernel(x_ref, buf_ref, h0_ref, z_ref, cw_ref, cb_ref, wa_ref, ba_ref, wx_ref, bx_ref,
                  lam_ref, o_ref, hs_ref, *, nb, t):
    x3 = x_ref[...].reshape(nb, t, BRANCH_W)
    ext = jnp.concatenate([buf_ref[...], x3], axis=1).reshape(nb * (CONV_PAD + t), BRANCH_W)

    def pick(x):
        return x.reshape(nb, CONV_PAD + t, BRANCH_W)[:, CONV_PAD:].reshape(nb * t, BRANCH_W)

    xc = _conv_taps(ext, pick, cw_ref, cb_ref)
    a, b = _lru_coeffs(xc, wa_ref, ba_ref, wx_ref, bx_ref, lam_ref)
    a, b = _group_scan(a, b)
    hs = a * h0_ref[...] + b
    hs_ref[...] = hs
    o_ref[...] = (hs * _silu(z_ref[...])).astype(o_ref.dtype)


def _lru_weight_specs(nargs_map):
    full = lambda shape: pl.BlockSpec(shape, lambda *a: (0,) * len(shape))
    del nargs_map
    return [full((CONV_W, BRANCH_W)), full((1, BRANCH_W)),
            full((LRU_BLOCKS, LRU_BW, LRU_BW)), full((1, BRANCH_W)),
            full((LRU_BLOCKS, LRU_BW, LRU_BW)), full((1, BRANCH_W)),
            full((1, BRANCH_W))]


def _lru_weights(conv_w, conv_b, wa, ba, wx, bx, lam):
    r = lambda v: v.reshape(1, BRANCH_W)
    return (conv_w, r(conv_b), wa, r(ba), wx, r(bx), r(lam))


def _lru_prompt(proj, weights, *, batch, seq_len, tm=512):
    m = proj.shape[0]
    tps = seq_len // tm
    per = tm // CONV_PAD
    return pl.pallas_call(
        functools.partial(_lru_p_kernel, tm=tm, tps=tps),
        out_shape=(jax.ShapeDtypeStruct((m, BRANCH_W), BF16),
                   jax.ShapeDtypeStruct((batch, 1, BRANCH_W), F32)),
        grid=(m // tm,),
        in_specs=[pl.BlockSpec((tm, BRANCH_W), lambda i: (i, COL_X)),
                  pl.BlockSpec((CONV_PAD, BRANCH_W),
                               lambda i: (jnp.maximum(i * per - 1, 0), COL_X)),
                  pl.BlockSpec((tm, BRANCH_W), lambda i: (i, COL_Z + 2))]
        + _lru_weight_specs(None),
        out_specs=(pl.BlockSpec((tm, BRANCH_W), lambda i: (i, 0)),
                   pl.BlockSpec((None, 1, BRANCH_W), lambda i: (i // tps, 0, 0))),
        scratch_shapes=[pltpu.VMEM((tm, BRANCH_W), F32), pltpu.VMEM((tm, BRANCH_W), F32),
                        pltpu.VMEM((1, BRANCH_W), F32)],
        compiler_params=_params("arbitrary"),
        name="lru_prompt",
    )(proj, proj, proj, *weights)


def _lru_sample(proj, buf_padded, h0_rows, weights, *, t, nb=32):
    m = proj.shape[0]
    rows = nb * t
    return pl.pallas_call(
        functools.partial(_lru_s_kernel, nb=nb, t=t),
        out_shape=(jax.ShapeDtypeStruct((m, BRANCH_W), BF16),
                   jax.ShapeDtypeStruct((m, BRANCH_W), F32)),
        grid=(m // rows,),
        in_specs=[pl.BlockSpec((rows, BRANCH_W), lambda i: (i, COL_X)),
                  pl.BlockSpec((nb, CONV_PAD, BRANCH_W), lambda i: (i, 0, 0)),
                  pl.BlockSpec((rows, BRANCH_W), lambda i: (i, 0)),
                  pl.BlockSpec((rows, BRANCH_W), lambda i: (i, COL_Z + 2))]
        + _lru_weight_specs(None),
        out_specs=(pl.BlockSpec((rows, BRANCH_W), lambda i: (i, 0)),
                   pl.BlockSpec((rows, BRANCH_W), lambda i: (i, 0))),
        compiler_params=_params("parallel"),
        name="lru_sample",
    )(proj, buf_padded, h0_rows, proj, *weights)


def _mem_attn_kernel(q_ref, k_ref, v_ref, z_ref, o_ref):
    q = q_ref[...].astype(BF16)
    k = k_ref[...].astype(BF16)
    v = v_ref[...].astype(BF16)
    z = z_ref[...]
    for h in range(M_HEADS):
        c0, c1 = h * M_HD, (h + 1) * M_HD
        s = lax.dot_general(q[:, c0:c1], k[:, c0:c1], _NT, preferred_element_type=F32)
        s = s * (M_HD ** -0.5)
        p = jnp.exp(s - jnp.max(s, axis=-1, keepdims=True))
        den = jnp.sum(p, axis=-1, keepdims=True)
        o = jnp.dot((p / den).astype(BF16), v[:, c0:c1], preferred_element_type=F32)
        o_ref[:, c0:c1] = (o * _silu(z[:, c0:c1])).astype(o_ref.dtype)


def _mem_attn(proj, mem_k, mem_v, layer, *, n_seq, rows_per_seq, tq):
    n_mem = mem_k.shape[1] // n_seq
    nq = rows_per_seq // tq
    kv_spec = pl.BlockSpec((None, n_mem, BRANCH_W), lambda b, qi: (layer, b, 0))
    return pl.pallas_call(
        _mem_attn_kernel,
        out_shape=jax.ShapeDtypeStruct((n_seq * rows_per_seq, BRANCH_W), BF16),
        grid=(n_seq, nq),
        in_specs=[pl.BlockSpec((tq, BRANCH_W), lambda b, qi: (b * nq + qi, COL_QD)),
                  kv_spec, kv_spec,
                  pl.BlockSpec((tq, BRANCH_W), lambda b, qi: (b * nq + qi, COL_Z + 3))],
        out_specs=pl.BlockSpec((tq, BRANCH_W), lambda b, qi: (b * nq + qi, 0)),
        compiler_params=_params("parallel", "parallel"),
        name="mem_attn",
    )(proj, mem_k, mem_v, proj)


def _mem_attn_s_kernel(q_ref, k_ref, v_ref, z_ref, o_ref, *, nb, t):
    n_mem = k_ref.shape[1]
    rows = M_HEADS * t
    col = lax.broadcasted_iota(jnp.int32, (rows, n_mem * M_HEADS), 1)
    row = lax.broadcasted_iota(jnp.int32, (rows, n_mem * M_HEADS), 0)
    own_head = (col % M_HEADS) == (row // t)
    for b in range(nb):
        q = q_ref[b * t:(b + 1) * t, :].astype(BF16)
        qs = jnp.concatenate([q[:, h * M_HD:(h + 1) * M_HD] for h in range(M_HEADS)], axis=0)
        k2 = k_ref[b].reshape(n_mem * M_HEADS, M_HD).astype(BF16)
        v2 = v_ref[b].reshape(n_mem * M_HEADS, M_HD).astype(BF16)
        s = lax.dot_general(qs, k2, _NT, preferred_element_type=F32) * (M_HD ** -0.5)
        s = jnp.where(own_head, s, NEG)
        p = jnp.exp(s - jnp.max(s, axis=-1, keepdims=True))
        den = jnp.sum(p, axis=-1, keepdims=True)
        o = jnp.dot((p / den).astype(BF16), v2, preferred_element_type=F32)
        for h in range(M_HEADS):
            c0, c1 = h * M_HD, (h + 1) * M_HD
            zh = z_ref[b * t:(b + 1) * t, c0:c1]
            o_ref[b * t:(b + 1) * t, c0:c1] = (o[h * t:(h + 1) * t] * _silu(zh)).astype(o_ref.dtype)


def _mem_attn_sample(proj, cache_mem_k, cache_mem_v, layer, *, t, nb=8):
    _, n_seq, n_mem, _, _ = cache_mem_k.shape
    kv_spec = pl.BlockSpec((None, nb, n_mem, M_HEADS, M_HD), lambda i: (layer, i, 0, 0, 0))
    return pl.pallas_call(
        functools.partial(_mem_attn_s_kernel, nb=nb, t=t),
        out_shape=jax.ShapeDtypeStruct((n_seq * t, BRANCH_W), BF16),
        grid=(n_seq // nb,),
        in_specs=[pl.BlockSpec((nb * t, BRANCH_W), lambda i: (i, COL_QD)),
                  kv_spec, kv_spec,
                  pl.BlockSpec((nb * t, BRANCH_W), lambda i: (i, COL_Z + 3))],
        out_specs=pl.BlockSpec((nb * t, BRANCH_W), lambda i: (i, 0)),
        compiler_params=_params("parallel"),
        name="mem_attn_sample",
    )(proj, cache_mem_k, cache_mem_v, proj)


def _merge_kernel(*refs):
    h_refs, g_refs = refs[:N_BRANCH], refs[N_BRANCH:2 * N_BRANCH]
    w_ref, o_ref = refs[2 * N_BRANCH:]
    acc = None
    for n in range(N_BRANCH):
        up = jnp.dot(h_refs[n][...], w_ref[n], preferred_element_type=F32)
        term = jax.nn.sigmoid(g_refs[n][...]) * up
        acc = term if acc is None else acc + term
    o_ref[...] = acc.astype(o_ref.dtype)


def _merge(h_a, h_b, h_c, h_d, proj, w_branch_bf, layer, tm=1024, tn=512):
    m = h_a.shape[0]
    tm = min(tm, m)
    h_spec = pl.BlockSpec((tm, BRANCH_W), lambda i, j: (i, 0))
    gate_col0 = COL_G * BRANCH_W // tn
    g_specs = [pl.BlockSpec((tm, tn), lambda i, j, n=n: (i, gate_col0 + n * (D_MODEL // tn) + j))
               for n in range(N_BRANCH)]
    return pl.pallas_call(
        _merge_kernel,
        out_shape=jax.ShapeDtypeStruct((m, D_MODEL), BF16),
        grid=(m // tm, D_MODEL // tn),
        in_specs=[h_spec] * N_BRANCH + g_specs + [
            pl.BlockSpec((None, N_BRANCH, BRANCH_W, tn), lambda i, j: (layer, 0, 0, j))],
        out_specs=pl.BlockSpec((tm, tn), lambda i, j: (i, j)),
        compiler_params=_params("parallel", "parallel"),
        name="merge",
    )(h_a, h_b, h_c, h_d, proj, proj, proj, proj, w_branch_bf)


def _layer_dense_in(x, ln_g, w_in, layer):
    xn = _rmsnorm(x, ln_g, EPS, BF16)
    return _matmul(xn, w_in, layer)


def _layer_dense_out(x, h_a, h_b, h_c, h_d, proj, w_branch_bf, w_out, layer):
    merged = _merge(h_a, h_b, h_c, h_d, proj, w_branch_bf, layer)
    return _matmul(merged, w_out, layer, res=x)


def kernel(x_prompt, x_sample, cache_k, cache_v, cache_mem_k, cache_mem_v, state_pool, state_conv, state_h, page_table, mem_prompt, ln_g, w_in, lam_q1, lam_k1, lam_q2, lam_k2, attn_sub_g, pool_w, pool_scale, conv_w, conv_b, lru_wa, lru_ba, lru_wx, lru_bx, lru_lambda, mem_g, w_mem_k, w_mem_v, w_branch, w_out, final_g):
    batch, seq_len, _ = x_prompt.shape
    n_seq, t, _ = x_sample.shape
    depth = w_in.shape[0]
    n_mem = mem_prompt.shape[1]
    past_len = page_table.shape[1] * PAGE_SIZE
    rope_tm = 256

    cos_p, sin_p = _rope_tables(jnp.arange(seq_len, dtype=jnp.int32))
    cos_s, sin_s = _rope_tables(past_len + jnp.arange(t, dtype=jnp.int32))
    cos_s = jnp.tile(cos_s, (rope_tm // t, 1))
    sin_s = jnp.tile(sin_s, (rope_tm // t, 1))

    xp = x_prompt.reshape(batch * seq_len, D_MODEL)
    xs = x_sample.reshape(n_seq * t, D_MODEL)
    mem_rows = mem_prompt.reshape(batch * n_mem, D_MODEL)

    w_branch_bf = w_branch.astype(BF16)

    outs = {name: [] for name in ("kp", "vp", "ks", "vs", "mk", "mv", "pp", "ps", "cp", "cs", "hp", "hs")}
    for l in range(depth):
        lam_init = 0.8 - 0.6 * math.exp(-0.3 * l)
        lam_vecs = jnp.stack([lam_q1[l], lam_k1[l], lam_q2[l], lam_k2[l]])
        subg = attn_sub_g[l].reshape(1, HEAD_W)
        lru_w = _lru_weights(conv_w[l], conv_b[l], lru_wa[l], lru_ba[l], lru_wx[l], lru_bx[l],
                             lru_lambda[l])

        mn = _rmsnorm(mem_rows, mem_g[l], EPS, BF16)
        mk = _matmul(mn, w_mem_k, l)
        mv = _matmul(mn, w_mem_v, l)
        proj = _layer_dense_in(xp, ln_g[l], w_in, l)
        k_f, v_f, k_bf, v_bf, q_st = _rope_call(proj, cos_p, sin_p, seq_len=seq_len, tm=rope_tm)
        h_a = _attn_prompt(q_st, k_bf, v_bf, proj, subg, lam_vecs, lam_init,
                           batch=batch, seq_len=seq_len)
        h_b = _pool_prompt(proj, pool_w, pool_scale[l], l, seq_len=seq_len)
        h_c, h_last = _lru_prompt(proj, lru_w, batch=batch, seq_len=seq_len)
        h_d = _mem_attn(proj, mk[None], mv[None], 0, n_seq=batch, rows_per_seq=seq_len, tq=512)
        proj3 = proj.reshape(batch, seq_len, -1)
        outs["pp"].append(proj3[:, seq_len - POOL_BUF:, COL_U * BRANCH_W:(COL_U + 1) * BRANCH_W])
        outs["cp"].append(proj3[:, seq_len - (CONV_W - 1):, COL_X * BRANCH_W:(COL_X + 1) * BRANCH_W])
        xp = _layer_dense_out(xp, h_a, h_b, h_c, h_d, proj, w_branch_bf, w_out, l)
        outs["kp"].append(k_f.reshape(batch, 2 * A_HEADS, A_HD, seq_len))
        outs["vp"].append(v_f.reshape(batch, seq_len, A_HEADS, HEAD_W))
        outs["mk"].append(mk.reshape(batch, n_mem, M_HEADS, M_HD))
        outs["mv"].append(mv.reshape(batch, n_mem, M_HEADS, M_HD))
        outs["hp"].append(h_last.reshape(batch, BRANCH_W))

        proj = _layer_dense_in(xs, ln_g[l], w_in, l)
        k_f, v_f, k_bf, v_bf, q_bf = _rope_call(proj, cos_s, sin_s, seq_len=0, tm=rope_tm)
        h_a = _attn_sample(q_bf, k_bf, v_bf, cache_k, cache_v, l, page_table, proj, subg,
                           lam_vecs, lam_init, n_seq=n_seq, t=t)
        pool_buf = jnp.pad(state_pool[l], ((0, 0), (POOL_PAD - POOL_BUF, 0), (0, 0)))
        h_b = _pool_sample(proj, pool_buf, pool_w, pool_scale[l], l, t=t, past_len=past_len)
        conv_buf = jnp.pad(state_conv[l], ((0, 0), (CONV_PAD - (CONV_W - 1), 0), (0, 0)))
        h0_rows = jnp.repeat(state_h[l], t, axis=0)
        h_c, hs_rows = _lru_sample(proj, conv_buf, h0_rows, lru_w, t=t)
        h_d = _mem_attn_sample(proj, cache_mem_k, cache_mem_v, l, t=t)
        proj3 = proj.reshape(n_seq, t, -1)
        u_s = proj3[:, :, COL_U * BRANCH_W:(COL_U + 1) * BRANCH_W]
        x_s = proj3[:, :, COL_X * BRANCH_W:(COL_X + 1) * BRANCH_W]
        outs["ps"].append(jnp.concatenate([state_pool[l], u_s], axis=1)[:, -POOL_BUF:])
        outs["cs"].append(jnp.concatenate([state_conv[l], x_s], axis=1)[:, -(CONV_W - 1):])
        xs = _layer_dense_out(xs, h_a, h_b, h_c, h_d, proj, w_branch_bf, w_out, l)
        outs["ks"].append(k_f.reshape(n_seq, t, 2 * A_HEADS, A_HD))
        outs["vs"].append(v_f.reshape(n_seq, t, A_HEADS, HEAD_W))
        outs["hs"].append(hs_rows.reshape(n_seq, t, BRANCH_W)[:, t - 1])

    y_prompt = _rmsnorm(xp, final_g, EPS, F32).reshape(batch, seq_len, D_MODEL)
    y_sample = _rmsnorm(xs, final_g, EPS, F32).reshape(n_seq, t, D_MODEL)
    st = lambda name: jnp.stack(outs[name])
    k_prompt = jnp.transpose(st("kp"), (0, 1, 4, 2, 3))
    return (y_prompt, y_sample, k_prompt, st("vp"), st("ks"), st("vs"), st("mk"), st("mv"),
            st("pp"), st("ps"), st("cp"), st("cs"), st("hp"), st("hs"))
```

```python
import functools
import math

import jax
import jax.numpy as jnp
from jax import lax
from jax.experimental import pallas as pl
from jax.experimental.pallas import tpu as pltpu

F32 = jnp.float32
BF16 = jnp.bfloat16

D_MODEL = 2048
BRANCH_W = 1024
A_HEADS = 8
A_HD = 64
HEAD_W = 2 * A_HD
ROPE_THETA = 10000.0
SUBLN_EPS = 1e-5
POOL_WINDOWS = (2, 4, 8, 16)
POOL_GW = BRANCH_W // len(POOL_WINDOWS)
POOL_BUF = max(POOL_WINDOWS) - 1
POOL_PAD = 16
LRU_BLOCKS = 8
LRU_BW = BRANCH_W // LRU_BLOCKS
CONV_W = 4
CONV_PAD = 8
LRU_C = 8.0
M_HEADS = 4
M_HD = BRANCH_W // M_HEADS
N_BRANCH = 4
PAGE_SIZE = 128
EPS = 1e-6
NEG = -1e30
SUBLANES = 8
LOG2_E = math.log2(math.e)

COL_Q, COL_K, COL_V, COL_U, COL_X, COL_QD = 0, 1, 2, 3, 4, 5
COL_Z = 6
COL_G = 10

VMEM_LIMIT = 56 * 1024 * 1024


def _params(*sem):
    return pltpu.CompilerParams(dimension_semantics=sem, vmem_limit_bytes=VMEM_LIMIT)


def _silu(z):
    return z * jax.nn.sigmoid(z)


def _rmsnorm_kernel(x_ref, g_ref, o_ref, *, eps):
    x = x_ref[...]
    y = x * lax.rsqrt(jnp.mean(x * x, axis=-1, keepdims=True) + eps)
    o_ref[...] = (y * g_ref[...]).astype(o_ref.dtype)


def _rmsnorm(x, g, eps, out_dtype, tm=512):
    m, d = x.shape
    return pl.pallas_call(
        functools.partial(_rmsnorm_kernel, eps=eps),
        out_shape=jax.ShapeDtypeStruct((m, d), out_dtype),
        grid=(m // tm,),
        in_specs=[pl.BlockSpec((tm, d), lambda i: (i, 0)),
                  pl.BlockSpec((1, d), lambda i: (0, 0))],
        out_specs=pl.BlockSpec((tm, d), lambda i: (i, 0)),
        compiler_params=_params("parallel"),
        name="rmsnorm",
    )(x, g.reshape(1, d))


def _mm_kernel(a_ref, w_ref, *rest, has_res):
    if has_res:
        r_ref, o_ref, wb_ref = rest
    else:
        o_ref, wb_ref = rest

    @pl.when(pl.program_id(1) == 0)
    def _():
        wb_ref[...] = w_ref[...].astype(BF16)

    acc = jnp.dot(a_ref[...], wb_ref[...], preferred_element_type=F32)
    if has_res:
        acc = r_ref[...] + acc
    o_ref[...] = acc


def _matmul(a, w, layer, res=None, tm=1024, tn=1024):
    m, k = a.shape
    n = w.shape[2]
    tm = min(tm, m)
    in_specs = [pl.BlockSpec((tm, k), lambda j, i: (i, 0)),
                pl.BlockSpec((None, k, tn), lambda j, i: (layer, 0, j))]
    args = [a, w]
    if res is not None:
        in_specs.append(pl.BlockSpec((tm, tn), lambda j, i: (i, j)))
        args.append(res)
    return pl.pallas_call(
        functools.partial(_mm_kernel, has_res=res is not None),
        out_shape=jax.ShapeDtypeStruct((m, n), F32),
        grid=(n // tn, m // tm),
        in_specs=in_specs,
        out_specs=pl.BlockSpec((tm, tn), lambda j, i: (i, j)),
        scratch_shapes=[pltpu.VMEM((k, tn), BF16)],
        compiler_params=_params("arbitrary", "arbitrary"),
        name="matmul",
    )(*args)


def _rope(x, cos, sin_signed):
    lane = lax.broadcasted_iota(jnp.int32, cos.shape, 1)
    first_half = (lane % A_HD) < (A_HD // 2)
    slabs = []
    for h in range(x.shape[1] // HEAD_W):
        xh = x[:, h * HEAD_W:(h + 1) * HEAD_W]
        partner = jnp.where(first_half,
                            pltpu.roll(xh, HEAD_W - A_HD // 2, 1),
                            pltpu.roll(xh, A_HD // 2, 1))
        slabs.append(xh * cos + partner * sin_signed)
    return jnp.concatenate(slabs, axis=1)


def _rope_kernel(q_ref, k_ref, v_ref, cos_ref, sin_ref,
                 ko_ref, vo_ref, kb_ref, vb_ref, qo_ref, *, stacked_q):
    cos = cos_ref[...]
    sin = sin_ref[...]
    k = _rope(k_ref[...], cos, sin)
    ko_ref[...] = k.T if stacked_q else k
    kb_ref[...] = k.astype(BF16)
    v = v_ref[...]
    vo_ref[...] = v
    vb_ref[...] = v.astype(BF16)
    q = _rope(q_ref[...], cos, sin) * (A_HD ** -0.5 * LOG2_E)
    if stacked_q:
        tm = q.shape[0]
        lane = lax.broadcasted_iota(jnp.int32, (tm, HEAD_W), 1)
        for h in range(A_HEADS):
            qh = q[:, h * HEAD_W:(h + 1) * HEAD_W]
            qo_ref[h, 0] = jnp.where(lane < A_HD, qh, 0.0).astype(BF16)
            qo_ref[h, 1] = jnp.where(lane >= A_HD, qh, 0.0).astype(BF16)
    else:
        qo_ref[...] = q.astype(BF16)


def _rope_call(proj, cos, sin_signed, *, seq_len, tm):
    m = proj.shape[0]
    stacked = seq_len > 0
    if stacked:
        tps = seq_len // tm
        tab_map = lambda i: (i % tps, 0)
        q_shape = jax.ShapeDtypeStruct((m // seq_len, A_HEADS, 2, seq_len, HEAD_W), BF16)
        q_spec = pl.BlockSpec((None, A_HEADS, 2, tm, HEAD_W),
                              lambda i: (i // tps, 0, 0, i % tps, 0))
        k_shape = jax.ShapeDtypeStruct((m // seq_len, BRANCH_W, seq_len), F32)
        k_spec = pl.BlockSpec((None, BRANCH_W, tm), lambda i: (i // tps, 0, i % tps))
    else:
        tab_map = lambda i: (0, 0)
        q_shape = jax.ShapeDtypeStruct((m, BRANCH_W), BF16)
        q_spec = pl.BlockSpec((tm, BRANCH_W), lambda i: (i, 0))
        k_shape = jax.ShapeDtypeStruct((m, BRANCH_W), F32)
        k_spec = pl.BlockSpec((tm, BRANCH_W), lambda i: (i, 0))
    col = lambda c: pl.BlockSpec((tm, BRANCH_W), lambda i, c=c: (i, c))
    row = pl.BlockSpec((tm, BRANCH_W), lambda i: (i, 0))
    return pl.pallas_call(
        functools.partial(_rope_kernel, stacked_q=stacked),
        out_shape=(k_shape,
                   jax.ShapeDtypeStruct((m, BRANCH_W), F32),
                   jax.ShapeDtypeStruct((m, BRANCH_W), BF16),
                   jax.ShapeDtypeStruct((m, BRANCH_W), BF16),
                   q_shape),
        grid=(m // tm,),
        in_specs=[col(COL_Q), col(COL_K), col(COL_V),
                  pl.BlockSpec((tm, HEAD_W), tab_map),
                  pl.BlockSpec((tm, HEAD_W), tab_map)],
        out_specs=(k_spec, row, row, row, q_spec),
        compiler_params=_params("parallel"),
        name="rope",
    )(proj, proj, proj, cos, sin_signed)


def _rope_tables(pos):
    half = A_HD // 2
    inv = 1.0 / (ROPE_THETA ** (jnp.arange(half, dtype=F32) / half))
    ang = pos.astype(F32)[:, None] * inv[None, :]
    cos, sin = jnp.cos(ang), jnp.sin(ang)
    cos = jnp.concatenate([cos, cos], axis=-1)
    sin = jnp.concatenate([-sin, sin], axis=-1)
    return jnp.tile(cos, (1, 2)), jnp.tile(sin, (1, 2))


def _lambda(lam_ref, lam_init):
    lv = lam_ref[...]
    l1 = jnp.sum(lv[0:1] * lv[1:2], axis=-1, keepdims=True)
    l2 = jnp.sum(lv[2:3] * lv[3:4], axis=-1, keepdims=True)
    return jnp.exp(l1) - jnp.exp(l2) + lam_init


def _diff_head_out(o1, o2, lam, subg, z, lam_init):
    d = o1 - lam * o2
    y = d * lax.rsqrt(jnp.mean(d * d, axis=-1, keepdims=True) + SUBLN_EPS) * subg
    return (y * (1.0 - lam_init)) * _silu(z)


_NT = (((1,), (1,)), ((), ()))
LANES = 128


def _lane_block_sum(p):
    out = p[:, :LANES]
    for c in range(1, p.shape[1] // LANES):
        out = out + p[:, c * LANES:(c + 1) * LANES]
    return out


def _softmax_block(s, m_prev):
    m_new = jnp.maximum(m_prev, jnp.max(s, axis=-1, keepdims=True))
    alpha = jnp.exp2(m_prev - m_new)
    p = jnp.exp2(s - jnp.tile(m_new, (1, s.shape[1] // LANES)))
    return m_new, alpha, p


def _attn_p_kernel(q_ref, k_ref, v_ref, z_ref, subg_ref, lam_ref, o_ref,
                   m_ref, l_ref, acc_ref, *, tq, tk, hpb, lam_init):
    qi = pl.program_id(2)
    rows = 2 * tq

    def block(start, diagonal):
        for j in range(hpb):
            cols = slice(j * HEAD_W, (j + 1) * HEAD_W)
            q = q_ref[j].reshape(rows, HEAD_W)
            s = lax.dot_general(q, k_ref[pl.ds(start, tk), cols], _NT,
                                preferred_element_type=F32)
            v = v_ref[pl.ds(start, tk), cols]
            if diagonal:
                r = lax.broadcasted_iota(jnp.int32, s.shape, 0)
                c = lax.broadcasted_iota(jnp.int32, s.shape, 1)
                s = jnp.where(start + c <= qi * tq + r % tq, s, NEG)
                m_new = jnp.broadcast_to(jnp.max(s, axis=-1, keepdims=True), (rows, LANES))
                p = jnp.exp2(s - jnp.tile(m_new, (1, tk // LANES)))
                l_ref[j] = _lane_block_sum(p)
                acc_ref[j] = jnp.dot(p.astype(BF16), v, preferred_element_type=F32)
            else:
                m_new, alpha, p = _softmax_block(s, m_ref[j])
                l_ref[j] = alpha * l_ref[j] + _lane_block_sum(p)
                acc_ref[j] = alpha * acc_ref[j] + jnp.dot(
                    p.astype(BF16), v, preferred_element_type=F32)
            m_ref[j] = m_new

    n_full = (qi * tq) // tk
    block(pl.multiple_of(n_full * tk, tk), True)

    def body(ki, carry):
        block(pl.multiple_of(ki * tk, tk), False)
        return carry

    lax.fori_loop(0, n_full, body, 0)

    lam = _lambda(lam_ref, lam_init)
    for j in range(hpb):
        cols = slice(j * HEAD_W, (j + 1) * HEAD_W)
        o = acc_ref[j] / jnp.sum(l_ref[j], axis=-1, keepdims=True)
        out = _diff_head_out(o[:tq], o[tq:], lam, subg_ref[...], z_ref[:, cols], lam_init)
        o_ref[:, cols] = out.astype(o_ref.dtype)


def _attn_prompt(q_st, k_bf, v_bf, proj, subg, lam_vecs, lam_init, *, batch, seq_len,
                 tq=256, tk=256, hpb=8):
    nq = seq_len // tq
    gw = hpb * HEAD_W
    zcol = COL_Z * (BRANCH_W // gw)
    kv_spec = pl.BlockSpec((seq_len, gw), lambda b, g, qi: (b, g))
    return pl.pallas_call(
        functools.partial(_attn_p_kernel, tq=tq, tk=tk, hpb=hpb, lam_init=lam_init),
        out_shape=jax.ShapeDtypeStruct((batch * seq_len, BRANCH_W), BF16),
        grid=(batch, A_HEADS // hpb, nq),
        in_specs=[pl.BlockSpec((None, hpb, 2, tq, HEAD_W), lambda b, g, qi: (b, g, 0, qi, 0)),
                  kv_spec, kv_spec,
                  pl.BlockSpec((tq, gw), lambda b, g, qi: (b * nq + qi, zcol + g)),
                  pl.BlockSpec((1, HEAD_W), lambda b, g, qi: (0, 0)),
                  pl.BlockSpec((4, A_HD), lambda b, g, qi: (0, 0))],
        out_specs=pl.BlockSpec((tq, gw), lambda b, g, qi: (b * nq + qi, g)),
        scratch_shapes=[pltpu.VMEM((hpb, 2 * tq, LANES), F32),
                        pltpu.VMEM((hpb, 2 * tq, LANES), F32),
                        pltpu.VMEM((hpb, 2 * tq, HEAD_W), F32)],
        compiler_params=_params("parallel", "parallel", "arbitrary"),
        name="attn_prompt",
    )(q_st, k_bf, v_bf, proj, subg, lam_vecs)


def _attn_s_kernel(pt_ref, q_ref, kn_ref, vn_ref, *rest, t, pps, lam_init):
    del pt_ref
    kc_refs, vc_refs = rest[:pps], rest[pps:2 * pps]
    (z_ref, subg_ref, lam_ref, o_ref,
     qbd_ref, knp_ref, vnp_ref, m_ref, l_ref, acc_ref) = rest[2 * pps:]
    p_idx = pl.program_id(1)
    hrows = 2 * t

    def stats(s):
        m_new, alpha, p = _softmax_block(s, m_ref[...])
        l_ref[...] = alpha * l_ref[...] + _lane_block_sum(p)
        m_ref[...] = m_new
        return alpha, p.astype(BF16)

    @pl.when((p_idx == 0) & (pl.program_id(0) == 0))
    def _():
        knp_ref[...] = jnp.zeros(knp_ref.shape, BF16)
        vnp_ref[...] = jnp.zeros(vnp_ref.shape, BF16)

    @pl.when(p_idx == 0)
    def _():
        q = q_ref[...]
        lane = lax.broadcasted_iota(jnp.int32, q.shape, 1)
        for hc in range(2 * A_HEADS):
            keep = (lane >= hc * A_HD) & (lane < (hc + 1) * A_HD)
            qbd_ref[hc * t:(hc + 1) * t, :] = jnp.where(keep, q, jnp.zeros_like(q))
        m_ref[...] = jnp.full(m_ref.shape, NEG, F32)
        l_ref[...] = jnp.zeros(l_ref.shape, F32)
        knp_ref[0:t, :] = kn_ref[...]
        vnp_ref[0:t, :] = vn_ref[...]
        s = lax.dot_general(qbd_ref[...], knp_ref[...], _NT, preferred_element_type=F32)
        r = lax.broadcasted_iota(jnp.int32, s.shape, 0)
        c = lax.broadcasted_iota(jnp.int32, s.shape, 1)
        s = jnp.where(c <= (r % t), s, NEG)
        _, pb = stats(s)
        acc_ref[...] = jnp.concatenate(
            [jnp.dot(pb[h * hrows:(h + 1) * hrows], vnp_ref[:, h * HEAD_W:(h + 1) * HEAD_W],
                     preferred_element_type=F32) for h in range(A_HEADS)], axis=0)

    kcat = jnp.concatenate(
        [r[...].reshape(BRANCH_W, PAGE_SIZE).astype(BF16) for r in kc_refs], axis=1)
    s = jnp.dot(qbd_ref[...], kcat, preferred_element_type=F32)
    alpha, pb = stats(s)
    parts = []
    for h in range(A_HEADS):
        vh = jnp.concatenate(
            [r[pl.ds(h, PAGE_SIZE, stride=A_HEADS), :].astype(BF16) for r in vc_refs], axis=0)
        parts.append(jnp.dot(pb[h * hrows:(h + 1) * hrows], vh, preferred_element_type=F32))
    acc_ref[...] = alpha * acc_ref[...] + jnp.concatenate(parts, axis=0)

    @pl.when(p_idx == pl.num_programs(1) - 1)
    def _():
        o = acc_ref[...] / jnp.sum(l_ref[...], axis=-1, keepdims=True)
        lam = _lambda(lam_ref, lam_init)
        subg = subg_ref[...]
        for h in range(A_HEADS):
            o1 = o[(2 * h) * t:(2 * h + 1) * t]
            o2 = o[(2 * h + 1) * t:(2 * h + 2) * t]
            zh = z_ref[:, h * HEAD_W:(h + 1) * HEAD_W]
            out = _diff_head_out(o1, o2, lam, subg, zh, lam_init)
            o_ref[:, h * HEAD_W:(h + 1) * HEAD_W] = out.astype(o_ref.dtype)


def _attn_sample(q_bf, k_bf, v_bf, cache_k, cache_v, layer, page_table, proj, subg, lam_vecs,
                 lam_init, *, n_seq, t, pps=16):
    n_pages = page_table.shape[1]
    rows = 2 * A_HEADS * t
    pt_flat = page_table.reshape(-1)
    kc = jnp.transpose(cache_k, (0, 1, 3, 4, 2))
    vc = cache_v.reshape(cache_v.shape[0], cache_v.shape[1], PAGE_SIZE * A_HEADS, HEAD_W)
    seq_spec = pl.BlockSpec((t, BRANCH_W), lambda b, p, pt: (b, 0))

    def page(i, ndim=3):
        return lambda b, p, pt: (layer, pt[b * n_pages + p * pps + i]) + (0,) * ndim

    k_specs = [pl.BlockSpec((None, None, 2 * A_HEADS, A_HD, PAGE_SIZE), page(i))
               for i in range(pps)]
    v_specs = [pl.BlockSpec((None, None, PAGE_SIZE * A_HEADS, HEAD_W), page(i, 2))
               for i in range(pps)]
    grid_spec = pltpu.PrefetchScalarGridSpec(
        num_scalar_prefetch=1,
        grid=(n_seq, n_pages // pps),
        in_specs=[seq_spec, seq_spec, seq_spec] + k_specs + v_specs + [
            pl.BlockSpec((t, BRANCH_W), lambda b, p, pt: (b, COL_Z)),
            pl.BlockSpec((1, HEAD_W), lambda b, p, pt: (0, 0)),
            pl.BlockSpec((4, A_HD), lambda b, p, pt: (0, 0))],
        out_specs=pl.BlockSpec((t, BRANCH_W), lambda b, p, pt: (b, 0)),
        scratch_shapes=[pltpu.VMEM((rows, BRANCH_W), BF16),
                        pltpu.VMEM((PAGE_SIZE, BRANCH_W), BF16),
                        pltpu.VMEM((PAGE_SIZE, BRANCH_W), BF16),
                        pltpu.VMEM((rows, LANES), F32), pltpu.VMEM((rows, LANES), F32),
                        pltpu.VMEM((rows, HEAD_W), F32)],
    )
    return pl.pallas_call(
        functools.partial(_attn_s_kernel, t=t, pps=pps, lam_init=lam_init),
        out_shape=jax.ShapeDtypeStruct((n_seq * t, BRANCH_W), BF16),
        grid_spec=grid_spec,
        compiler_params=_params("arbitrary", "arbitrary"),
        name="attn_sample",
    )(pt_flat, q_bf, k_bf, v_bf, *([kc] * pps), *([vc] * pps), proj, subg, lam_vecs)


def _window_sum(e, log2_win):
    s = e
    for k in range(log2_win):
        s = s + pltpu.roll(s, 1 << k, 0)
    return s


def _pool_groups(ext, pos, w_ref, scale_ref, z, o_ref, pick):
    for gi, win in enumerate(POOL_WINDOWS):
        c0, c1 = gi * POOL_GW, (gi + 1) * POOL_GW
        e = ext[:, c0:c1]
        cur = pick(e)
        tot = pick(_window_sum(e, gi + 1))
        cnt = jnp.minimum(pos + 1, win).astype(F32)
        mixed = tot / cnt - cur
        out = jnp.dot(mixed.astype(BF16), w_ref[gi].astype(BF16), preferred_element_type=F32)
        out = out * scale_ref[:, c0:c1]
        o_ref[:, c0:c1] = (out * _silu(z[:, c0:c1])).astype(o_ref.dtype)


def _pool_p_kernel(u_ref, prev_ref, z_ref, w_ref, scale_ref, o_ref, *, tm, tps):
    i = pl.program_id(0)
    first = (i % tps) == 0
    prev = jnp.where(first, 0.0, prev_ref[...])
    ext = jnp.concatenate([prev, u_ref[...]], axis=0)
    pos = (i % tps) * tm + lax.broadcasted_iota(jnp.int32, (tm, POOL_GW), 0)
    _pool_groups(ext, pos, w_ref, scale_ref, z_ref[...], o_ref, lambda x: x[POOL_PAD:])


def _pool_s_kernel(u_ref, buf_ref, z_ref, w_ref, scale_ref, o_ref, *, nb, t, past_len):
    u3 = u_ref[...].reshape(nb, t, BRANCH_W)
    ext = jnp.concatenate([buf_ref[...], u3], axis=1)
    ext = ext.reshape(nb * (POOL_PAD + t), BRANCH_W)
    pos = past_len + lax.broadcasted_iota(jnp.int32, (nb * t, POOL_GW), 0) % t

    def pick(x):
        x3 = x.reshape(nb, POOL_PAD + t, x.shape[-1])
        return x3[:, POOL_PAD:].reshape(nb * t, x.shape[-1])

    _pool_groups(ext, pos, w_ref, scale_ref, z_ref[...], o_ref, pick)


def _pool_prompt(proj, pool_w, pool_scale, layer, *, seq_len, tm=512):
    m = proj.shape[0]
    tps = seq_len // tm
    per = tm // POOL_PAD
    return pl.pallas_call(
        functools.partial(_pool_p_kernel, tm=tm, tps=tps),
        out_shape=jax.ShapeDtypeStruct((m, BRANCH_W), BF16),
        grid=(m // tm,),
        in_specs=[pl.BlockSpec((tm, BRANCH_W), lambda i: (i, COL_U)),
                  pl.BlockSpec((POOL_PAD, BRANCH_W),
                               lambda i: (jnp.maximum(i * per - 1, 0), COL_U)),
                  pl.BlockSpec((tm, BRANCH_W), lambda i: (i, COL_Z + 1)),
                  pl.BlockSpec((None,) + pool_w.shape[1:], lambda i: (layer, 0, 0, 0)),
                  pl.BlockSpec((1, BRANCH_W), lambda i: (0, 0))],
        out_specs=pl.BlockSpec((tm, BRANCH_W), lambda i: (i, 0)),
        compiler_params=_params("parallel"),
        name="pool_prompt",
    )(proj, proj, proj, pool_w, pool_scale.reshape(1, BRANCH_W))


def _pool_sample(proj, buf_padded, pool_w, pool_scale, layer, *, t, past_len, nb=32):
    m = proj.shape[0]
    return pl.pallas_call(
        functools.partial(_pool_s_kernel, nb=nb, t=t, past_len=past_len),
        out_shape=jax.ShapeDtypeStruct((m, BRANCH_W), BF16),
        grid=(m // (nb * t),),
        in_specs=[pl.BlockSpec((nb * t, BRANCH_W), lambda i: (i, COL_U)),
                  pl.BlockSpec((nb, POOL_PAD, BRANCH_W), lambda i: (i, 0, 0)),
                  pl.BlockSpec((nb * t, BRANCH_W), lambda i: (i, COL_Z + 1)),
                  pl.BlockSpec((None,) + pool_w.shape[1:], lambda i: (layer, 0, 0, 0)),
                  pl.BlockSpec((1, BRANCH_W), lambda i: (0, 0))],
        out_specs=pl.BlockSpec((nb * t, BRANCH_W), lambda i: (i, 0)),
        compiler_params=_params("parallel"),
        name="pool_sample",
    )(proj, buf_padded, proj, pool_w, pool_scale.reshape(1, BRANCH_W))


def _conv_taps(ext, pick, cw_ref, cb_ref):
    out = cb_ref[...] + pick(ext) * cw_ref[CONV_W - 1:CONV_W, :]
    for k in range(CONV_W - 1):
        shift = CONV_W - 1 - k
        out = out + pick(pltpu.roll(ext, shift, 0)) * cw_ref[k:k + 1, :]
    return out


def _block_diag_gate(xb, w_ref, b_ref):
    parts = [jnp.dot(xb[:, n * LRU_BW:(n + 1) * LRU_BW], w_ref[n].astype(BF16),
                     preferred_element_type=F32) for n in range(LRU_BLOCKS)]
    return jax.nn.sigmoid(jnp.concatenate(parts, axis=-1) + b_ref[...])


def _lru_coeffs(xc, wa_ref, ba_ref, wx_ref, bx_ref, lam_ref):
    xb = xc.astype(BF16)
    r = _block_diag_gate(xb, wa_ref, ba_ref)
    i = _block_diag_gate(xb, wx_ref, bx_ref)
    nl = -lam_ref[...]
    softplus = jnp.maximum(nl, 0.0) + jnp.log1p(jnp.exp(-jnp.abs(nl)))
    log_a = r * (-LRU_C * softplus)
    a = jnp.exp(log_a)
    b = jnp.sqrt(1.0 - a * a) * (i * xc)
    return a, b


def _group_scan(a, b):
    shape = a.shape
    grouped = (shape[0] // SUBLANES, SUBLANES, shape[1])
    a = a.reshape(grouped)
    b = b.reshape(grouped)
    row = lax.broadcasted_iota(jnp.int32, grouped, 1)
    d = 1
    while d < SUBLANES:
        valid = row >= d
        a_sh = pltpu.roll(a, d, 1)
        b_sh = pltpu.roll(b, d, 1)
        b = jnp.where(valid, a * b_sh + b, b)
        a = jnp.where(valid, a * a_sh, a)
        d *= 2
    return a.reshape(shape), b.reshape(shape)


def _lru_p_kernel(x_ref, prev_ref, z_ref, cw_ref, cb_ref, wa_ref, ba_ref, wx_ref, bx_ref,
                  lam_ref, o_ref, hl_ref, a_s, b_s, h_s, *, tm, tps):
    i = pl.program_id(0)
    first = (i % tps) == 0

    @pl.when(first)
    def _():
        h_s[...] = jnp.zeros(h_s.shape, F32)

    prev = jnp.where(first, 0.0, prev_ref[...])
    ext = jnp.concatenate([prev, x_ref[...]], axis=0)
    xc = _conv_taps(ext, lambda x: x[CONV_PAD:], cw_ref, cb_ref)
    a, b = _lru_coeffs(xc, wa_ref, ba_ref, wx_ref, bx_ref, lam_ref)
    a, b = _group_scan(a, b)
    a_s[...] = a
    b_s[...] = b

    def body(g, h):
        r0 = pl.multiple_of(g * SUBLANES, SUBLANES)
        hs = a_s[pl.ds(r0, SUBLANES), :] * h + b_s[pl.ds(r0, SUBLANES), :]
        b_s[pl.ds(r0, SUBLANES), :] = hs
        return hs[SUBLANES - 1:SUBLANES, :]

    h = lax.fori_loop(0, tm // SUBLANES, body, h_s[...])
    h_s[...] = h
    hl_ref[...] = h
    o_ref[...] = (b_s[...] * _silu(z_ref[...])).astype(o_ref.dtype)


def _lru_s_kernel(x_ref, buf_ref, h0_ref, z_ref, cw_ref, cb_ref, wa_ref, ba_ref, wx_ref, bx_ref,
                  lam_ref, o_ref, hs_ref, *, nb, t):
    x3 = x_ref[...].reshape(nb, t, BRANCH_W)
    ext = jnp.concatenate([buf_ref[...], x3], axis=1).reshape(nb * (CONV_PAD + t), BRANCH_W)

    def pick(x):
        return x.reshape(nb, CONV_PAD + t, BRANCH_W)[:, CONV_PAD:].reshape(nb * t, BRANCH_W)

    xc = _conv_taps(ext, pick, cw_ref, cb_ref)
    a, b = _lru_coeffs(xc, wa_ref, ba_ref, wx_ref, bx_ref, lam_ref)
    a, b = _group_scan(a, b)
    hs = a * h0_ref[...] + b
    hs_ref[...] = hs
    o_ref[...] = (hs * _silu(z_ref[...])).astype(o_ref.dtype)


def _lru_weight_specs(nargs_map):
    full = lambda shape: pl.BlockSpec(shape, lambda *a: (0,) * len(shape))
    del nargs_map
    return [full((CONV_W, BRANCH_W)), full((1, BRANCH_W)),
            full((LRU_BLOCKS, LRU_BW, LRU_BW)), full((1, BRANCH_W)),
            full((LRU_BLOCKS, LRU_BW, LRU_BW)), full((1, BRANCH_W)),
            full((1, BRANCH_W))]


def _lru_weights(conv_w, conv_b, wa, ba, wx, bx, lam):
    r = lambda v: v.reshape(1, BRANCH_W)
    return (conv_w, r(conv_b), wa, r(ba), wx, r(bx), r(lam))


def _lru_prompt(proj, weights, *, batch, seq_len, tm=512):
    m = proj.shape[0]
    tps = seq_len // tm
    per = tm // CONV_PAD
    return pl.pallas_call(
        functools.partial(_lru_p_kernel, tm=tm, tps=tps),
        out_shape=(jax.ShapeDtypeStruct((m, BRANCH_W), BF16),
                   jax.ShapeDtypeStruct((batch, 1, BRANCH_W), F32)),
        grid=(m // tm,),
        in_specs=[pl.BlockSpec((tm, BRANCH_W), lambda i: (i, COL_X)),
                  pl.BlockSpec((CONV_PAD, BRANCH_W),
                               lambda i: (jnp.maximum(i * per - 1, 0), COL_X)),
                  pl.BlockSpec((tm, BRANCH_W), lambda i: (i, COL_Z + 2))]
        + _lru_weight_specs(None),
        out_specs=(pl.BlockSpec((tm, BRANCH_W), lambda i: (i, 0)),
                   pl.BlockSpec((None, 1, BRANCH_W), lambda i: (i // tps, 0, 0))),
        scratch_shapes=[pltpu.VMEM((tm, BRANCH_W), F32), pltpu.VMEM((tm, BRANCH_W), F32),
                        pltpu.VMEM((1, BRANCH_W), F32)],
        compiler_params=_params("arbitrary"),
        name="lru_prompt",
    )(proj, proj, proj, *weights)


def _lru_sample(proj, buf_padded, h0_rows, weights, *, t, nb=32):
    m = proj.shape[0]
    rows = nb * t
    return pl.pallas_call(
        functools.partial(_lru_s_kernel, nb=nb, t=t),
        out_shape=(jax.ShapeDtypeStruct((m, BRANCH_W), BF16),
                   jax.ShapeDtypeStruct((m, BRANCH_W), F32)),
        grid=(m // rows,),
        in_specs=[pl.BlockSpec((rows, BRANCH_W), lambda i: (i, COL_X)),
                  pl.BlockSpec((nb, CONV_PAD, BRANCH_W), lambda i: (i, 0, 0)),
                  pl.BlockSpec((rows, BRANCH_W), lambda i: (i, 0)),
                  pl.BlockSpec((rows, BRANCH_W), lambda i: (i, COL_Z + 2))]
        + _lru_weight_specs(None),
        out_specs=(pl.BlockSpec((rows, BRANCH_W), lambda i: (i, 0)),
                   pl.BlockSpec((rows, BRANCH_W), lambda i: (i, 0))),
        compiler_params=_params("parallel"),
        name="lru_sample",
    )(proj, buf_padded, h0_rows, proj, *weights)


def _mem_attn_kernel(q_ref, k_ref, v_ref, z_ref, o_ref):
    q = q_ref[...].astype(BF16)
    k = k_ref[...].astype(BF16)
    v = v_ref[...].astype(BF16)
    z = z_ref[...]
    for h in range(M_HEADS):
        c0, c1 = h * M_HD, (h + 1) * M_HD
        s = lax.dot_general(q[:, c0:c1], k[:, c0:c1], _NT, preferred_element_type=F32)
        s = s * (M_HD ** -0.5)
        p = jnp.exp(s - jnp.max(s, axis=-1, keepdims=True))
        den = jnp.sum(p, axis=-1, keepdims=True)
        o = jnp.dot((p / den).astype(BF16), v[:, c0:c1], preferred_element_type=F32)
        o_ref[:, c0:c1] = (o * _silu(z[:, c0:c1])).astype(o_ref.dtype)


def _mem_attn(proj, mem_k, mem_v, layer, *, n_seq, rows_per_seq, tq):
    n_mem = mem_k.shape[1] // n_seq
    nq = rows_per_seq // tq
    kv_spec = pl.BlockSpec((None, n_mem, BRANCH_W), lambda b, qi: (layer, b, 0))
    return pl.pallas_call(
        _mem_attn_kernel,
        out_shape=jax.ShapeDtypeStruct((n_seq * rows_per_seq, BRANCH_W), BF16),
        grid=(n_seq, nq),
        in_specs=[pl.BlockSpec((tq, BRANCH_W), lambda b, qi: (b * nq + qi, COL_QD)),
                  kv_spec, kv_spec,
                  pl.BlockSpec((tq, BRANCH_W), lambda b, qi: (b * nq + qi, COL_Z + 3))],
        out_specs=pl.BlockSpec((tq, BRANCH_W), lambda b, qi: (b * nq + qi, 0)),
        compiler_params=_params("parallel", "parallel"),
        name="mem_attn",
    )(proj, mem_k, mem_v, proj)


def _mem_attn_s_kernel(q_ref, k_ref, v_ref, z_ref, o_ref, *, nb, t):
    n_mem = k_ref.shape[1]
    rows = M_HEADS * t
    col = lax.broadcasted_iota(jnp.int32, (rows, n_mem * M_HEADS), 1)
    row = lax.broadcasted_iota(jnp.int32, (rows, n_mem * M_HEADS), 0)
    own_head = (col % M_HEADS) == (row // t)
    for b in range(nb):
        q = q_ref[b * t:(b + 1) * t, :].astype(BF16)
        qs = jnp.concatenate([q[:, h * M_HD:(h + 1) * M_HD] for h in range(M_HEADS)], axis=0)
        k2 = k_ref[b].reshape(n_mem * M_HEADS, M_HD).astype(BF16)
        v2 = v_ref[b].reshape(n_mem * M_HEADS, M_HD).astype(BF16)
        s = lax.dot_general(qs, k2, _NT, preferred_element_type=F32) * (M_HD ** -0.5)
        s = jnp.where(own_head, s, NEG)
        p = jnp.exp(s - jnp.max(s, axis=-1, keepdims=True))
        den = jnp.sum(p, axis=-1, keepdims=True)
        o = jnp.dot((p / den).astype(BF16), v2, preferred_element_type=F32)
        for h in range(M_HEADS):
            c0, c1 = h * M_HD, (h + 1) * M_HD
            zh = z_ref[b * t:(b + 1) * t, c0:c1]
            o_ref[b * t:(b + 1) * t, c0:c1] = (o[h * t:(h + 1) * t] * _silu(zh)).astype(o_ref.dtype)


def _mem_attn_sample(proj, cache_mem_k, cache_mem_v, layer, *, t, nb=8):
    _, n_seq, n_mem, _, _ = cache_mem_k.shape
    kv_spec = pl.BlockSpec((None, nb, n_mem, M_HEADS, M_HD), lambda i: (layer, i, 0, 0, 0))
    return pl.pallas_call(
        functools.partial(_mem_attn_s_kernel, nb=nb, t=t),
        out_shape=jax.ShapeDtypeStruct((n_seq * t, BRANCH_W), BF16),
        grid=(n_seq // nb,),
        in_specs=[pl.BlockSpec((nb * t, BRANCH_W), lambda i: (i, COL_QD)),
                  kv_spec, kv_spec,
                  pl.BlockSpec((nb * t, BRANCH_W), lambda i: (i, COL_Z + 3))],
        out_specs=pl.BlockSpec((nb * t, BRANCH_W), lambda i: (i, 0)),
        compiler_params=_params("parallel"),
        name="mem_attn_sample",
    )(proj, cache_mem_k, cache_mem_v, proj)


def _merge_kernel(*refs):
    h_refs, g_refs = refs[:N_BRANCH], refs[N_BRANCH:2 * N_BRANCH]
    w_ref, o_ref = refs[2 * N_BRANCH:]
    acc = None
    for n in range(N_BRANCH):
        up = jnp.dot(h_refs[n][...], w_ref[n], preferred_element_type=F32)
        term = jax.nn.sigmoid(g_refs[n][...]) * up
        acc = term if acc is None else acc + term
    o_ref[...] = acc.astype(o_ref.dtype)


def _merge(h_a, h_b, h_c, h_d, proj, w_branch_bf, layer, tm=1024, tn=512):
    m = h_a.shape[0]
    tm = min(tm, m)
    h_spec = pl.BlockSpec((tm, BRANCH_W), lambda i, j: (i, 0))
    gate_col0 = COL_G * BRANCH_W // tn
    g_specs = [pl.BlockSpec((tm, tn), lambda i, j, n=n: (i, gate_col0 + n * (D_MODEL // tn) + j))
               for n in range(N_BRANCH)]
    return pl.pallas_call(
        _merge_kernel,
        out_shape=jax.ShapeDtypeStruct((m, D_MODEL), BF16),
        grid=(m // tm, D_MODEL // tn),
        in_specs=[h_spec] * N_BRANCH + g_specs + [
            pl.BlockSpec((None, N_BRANCH, BRANCH_W, tn), lambda i, j: (layer, 0, 0, j))],
        out_specs=pl.BlockSpec((tm, tn), lambda i, j: (i, j)),
        compiler_params=_params("parallel", "parallel"),
        name="merge",
    )(h_a, h_b, h_c, h_d, proj, proj, proj, proj, w_branch_bf)


def _layer_dense_in(x, ln_g, w_in, layer):
    xn = _rmsnorm(x, ln_g, EPS, BF16)
    return _matmul(xn, w_in, layer)


def _layer_dense_out(x, h_a, h_b, h_c, h_d, proj, w_branch_bf, w_out, layer):
    merged = _merge(h_a, h_b, h_c, h_d, proj, w_branch_bf, layer)
    return _matmul(merged, w_out, layer, res=x)


def kernel(x_prompt, x_sample, cache_k, cache_v, cache_mem_k, cache_mem_v, state_pool, state_conv, state_h, page_table, mem_prompt, ln_g, w_in, lam_q1, lam_k1, lam_q2, lam_k2, attn_sub_g, pool_w, pool_scale, conv_w, conv_b, lru_wa, lru_ba, lru_wx, lru_bx, lru_lambda, mem_g, w_mem_k, w_mem_v, w_branch, w_out, final_g):
    batch, seq_len, _ = x_prompt.shape
    n_seq, t, _ = x_sample.shape
    depth = w_in.shape[0]
    n_mem = mem_prompt.shape[1]
    past_len = page_table.shape[1] * PAGE_SIZE
    rope_tm = 256

    cos_p, sin_p = _rope_tables(jnp.arange(seq_len, dtype=jnp.int32))
    cos_s, sin_s = _rope_tables(past_len + jnp.arange(t, dtype=jnp.int32))
    cos_s = jnp.tile(cos_s, (rope_tm // t, 1))
    sin_s = jnp.tile(sin_s, (rope_tm // t, 1))

    xp = x_prompt.reshape(batch * seq_len, D_MODEL)
    xs = x_sample.reshape(n_seq * t, D_MODEL)
    mem_rows = mem_prompt.reshape(batch * n_mem, D_MODEL)

    w_branch_bf = w_branch.astype(BF16)

    outs = {name: [] for name in ("kp", "vp", "ks", "vs", "mk", "mv", "pp", "ps", "cp", "cs", "hp", "hs")}
    for l in range(depth):
        lam_init = 0.8 - 0.6 * math.exp(-0.3 * l)
        lam_vecs = jnp.stack([lam_q1[l], lam_k1[l], lam_q2[l], lam_k2[l]])
        subg = attn_sub_g[l].reshape(1, HEAD_W)
        lru_w = _lru_weights(conv_w[l], conv_b[l], lru_wa[l], lru_ba[l], lru_wx[l], lru_bx[l],
                             lru_lambda[l])

        mn = _rmsnorm(mem_rows, mem_g[l], EPS, BF16)
        mk = _matmul(mn, w_mem_k, l)
        mv = _matmul(mn, w_mem_v, l)
        proj = _layer_dense_in(xp, ln_g[l], w_in, l)
        k_f, v_f, k_bf, v_bf, q_st = _rope_call(proj, cos_p, sin_p, seq_len=seq_len, tm=rope_tm)
        h_a = _attn_prompt(q_st, k_bf, v_bf, proj, subg, lam_vecs, lam_init,
                           batch=batch, seq_len=seq_len)
        h_b = _pool_prompt(proj, pool_w, pool_scale[l], l, seq_len=seq_len)
        h_c, h_last = _lru_prompt(proj, lru_w, batch=batch, seq_len=seq_len)
        h_d = _mem_attn(proj, mk[None], mv[None], 0, n_seq=batch, rows_per_seq=seq_len, tq=1024)
        proj3 = proj.reshape(batch, seq_len, -1)
        outs["pp"].append(proj3[:, seq_len - POOL_BUF:, COL_U * BRANCH_W:(COL_U + 1) * BRANCH_W])
        outs["cp"].append(proj3[:, seq_len - (CONV_W - 1):, COL_X * BRANCH_W:(COL_X + 1) * BRANCH_W])
        xp = _layer_dense_out(xp, h_a, h_b, h_c, h_d, proj, w_branch_bf, w_out, l)
        outs["kp"].append(k_f.reshape(batch, 2 * A_HEADS, A_HD, seq_len))
        outs["vp"].append(v_f.reshape(batch, seq_len, A_HEADS, HEAD_W))
        outs["mk"].append(mk.reshape(batch, n_mem, M_HEADS, M_HD))
        outs["mv"].append(mv.reshape(batch, n_mem, M_HEADS, M_HD))
        outs["hp"].append(h_last.reshape(batch, BRANCH_W))

        proj = _layer_dense_in(xs, ln_g[l], w_in, l)
        k_f, v_f, k_bf, v_bf, q_bf = _rope_call(proj, cos_s, sin_s, seq_len=0, tm=rope_tm)
        h_a = _attn_sample(q_bf, k_bf, v_bf, cache_k, cache_v, l, page_table, proj, subg,
                           lam_vecs, lam_init, n_seq=n_seq, t=t)
        pool_buf = jnp.pad(state_pool[l], ((0, 0), (POOL_PAD - POOL_BUF, 0), (0, 0)))
        h_b = _pool_sample(proj, pool_buf, pool_w, pool_scale[l], l, t=t, past_len=past_len)
        conv_buf = jnp.pad(state_conv[l], ((0, 0), (CONV_PAD - (CONV_W - 1), 0), (0, 0)))
        h0_rows = jnp.repeat(state_h[l], t, axis=0)
        h_c, hs_rows = _lru_sample(proj, conv_buf, h0_rows, lru_w, t=t)
        h_d = _mem_attn_sample(proj, cache_mem_k, cache_mem_v, l, t=t)
        proj3 = proj.reshape(n_seq, t, -1)
        u_s = proj3[:, :, COL_U * BRANCH_W:(COL_U + 1) * BRANCH_W]
        x_s = proj3[:, :, COL_X * BRANCH_W:(COL_X + 1) * BRANCH_W]
        outs["ps"].append(jnp.concatenate([state_pool[l], u_s], axis=1)[:, -POOL_BUF:])
        outs["cs"].append(jnp.concatenate([state_conv[l], x_s], axis=1)[:, -(CONV_W - 1):])
        xs = _layer_dense_out(xs, h_a, h_b, h_c, h_d, proj, w_branch_bf, w_out, l)
        outs["ks"].append(k_f.reshape(n_seq, t, 2 * A_HEADS, A_HD))
        outs["vs"].append(v_f.reshape(n_seq, t, A_HEADS, HEAD_W))
        outs["hs"].append(hs_rows.reshape(n_seq, t, BRANCH_W)[:, t - 1])

    y_prompt = _rmsnorm(xp, final_g, EPS, F32).reshape(batch, seq_len, D_MODEL)
    y_sample = _rmsnorm(xs, final_g, EPS, F32).reshape(n_seq, t, D_MODEL)
    st = lambda name: jnp.stack(outs[name])
    k_prompt = jnp.transpose(st("kp"), (0, 1, 4, 2, 3))
    return (y_prompt, y_sample, k_prompt, st("vp"), st("ks"), st("vs"), st("mk"), st("mv"),
            st("pp"), st("ps"), st("cp"), st("cs"), st("hp"), st("hs"))
```

```python
import functools
import math

import jax
import jax.numpy as jnp
from jax import lax
from jax.experimental import pallas as pl
from jax.experimental.pallas import tpu as pltpu

F32 = jnp.float32
BF16 = jnp.bfloat16

D_MODEL = 2048
BRANCH_W = 1024
A_HEADS = 8
A_HD = 64
HEAD_W = 2 * A_HD
ROPE_THETA = 10000.0
SUBLN_EPS = 1e-5
POOL_WINDOWS = (2, 4, 8, 16)
POOL_GW = BRANCH_W // len(POOL_WINDOWS)
POOL_BUF = max(POOL_WINDOWS) - 1
POOL_PAD = 16
LRU_BLOCKS = 8
LRU_BW = BRANCH_W // LRU_BLOCKS
CONV_W = 4
CONV_PAD = 8
LRU_C = 8.0
M_HEADS = 4
M_HD = BRANCH_W // M_HEADS
N_BRANCH = 4
PAGE_SIZE = 128
EPS = 1e-6
NEG = -1e30
SUBLANES = 8
LOG2_E = math.log2(math.e)

COL_Q, COL_K, COL_V, COL_U, COL_X, COL_QD = 0, 1, 2, 3, 4, 5
COL_Z = 6
COL_G = 10

VMEM_LIMIT = 56 * 1024 * 1024


def _params(*sem):
    return pltpu.CompilerParams(dimension_semantics=sem, vmem_limit_bytes=VMEM_LIMIT)


def _silu(z):
    return z * jax.nn.sigmoid(z)


def _rmsnorm_kernel(x_ref, g_ref, o_ref, *, eps):
    x = x_ref[...]
    y = x * lax.rsqrt(jnp.mean(x * x, axis=-1, keepdims=True) + eps)
    o_ref[...] = (y * g_ref[...]).astype(o_ref.dtype)


def _rmsnorm(x, g, eps, out_dtype, tm=512):
    m, d = x.shape
    return pl.pallas_call(
        functools.partial(_rmsnorm_kernel, eps=eps),
        out_shape=jax.ShapeDtypeStruct((m, d), out_dtype),
        grid=(m // tm,),
        in_specs=[pl.BlockSpec((tm, d), lambda i: (i, 0)),
                  pl.BlockSpec((1, d), lambda i: (0, 0))],
        out_specs=pl.BlockSpec((tm, d), lambda i: (i, 0)),
        compiler_params=_params("parallel"),
        name="rmsnorm",
    )(x, g.reshape(1, d))


def _mm_kernel(a_ref, w_ref, *rest, has_res):
    if has_res:
        r_ref, o_ref, wb_ref = rest
    else:
        o_ref, wb_ref = rest

    @pl.when(pl.program_id(1) == 0)
    def _():
        wb_ref[...] = w_ref[...].astype(BF16)

    acc = jnp.dot(a_ref[...], wb_ref[...], preferred_element_type=F32)
    if has_res:
        acc = r_ref[...] + acc
    o_ref[...] = acc


def _matmul(a, w, layer, res=None, tm=1024, tn=1024):
    m, k = a.shape
    n = w.shape[2]
    tm = min(tm, m)
    in_specs = [pl.BlockSpec((tm, k), lambda j, i: (i, 0)),
                pl.BlockSpec((None, k, tn), lambda j, i: (layer, 0, j))]
    args = [a, w]
    if res is not None:
        in_specs.append(pl.BlockSpec((tm, tn), lambda j, i: (i, j)))
        args.append(res)
    return pl.pallas_call(
        functools.partial(_mm_kernel, has_res=res is not None),
        out_shape=jax.ShapeDtypeStruct((m, n), F32),
        grid=(n // tn, m // tm),
        in_specs=in_specs,
        out_specs=pl.BlockSpec((tm, tn), lambda j, i: (i, j)),
        scratch_shapes=[pltpu.VMEM((k, tn), BF16)],
        compiler_params=_params("arbitrary", "arbitrary"),
        name="matmul",
    )(*args)


def _rope(x, cos, sin_signed):
    lane = lax.broadcasted_iota(jnp.int32, cos.shape, 1)
    first_half = (lane % A_HD) < (A_HD // 2)
    slabs = []
    for h in range(x.shape[1] // HEAD_W):
        xh = x[:, h * HEAD_W:(h + 1) * HEAD_W]
        partner = jnp.where(first_half,
                            pltpu.roll(xh, HEAD_W - A_HD // 2, 1),
                            pltpu.roll(xh, A_HD // 2, 1))
        slabs.append(xh * cos + partner * sin_signed)
    return jnp.concatenate(slabs, axis=1)


def _rope_kernel(q_ref, k_ref, v_ref, cos_ref, sin_ref,
                 ko_ref, vo_ref, kb_ref, vb_ref, qo_ref, *, stacked_q):
    cos = cos_ref[...]
    sin = sin_ref[...]
    k = _rope(k_ref[...], cos, sin)
    ko_ref[...] = k.T if stacked_q else k
    kb_ref[...] = k.astype(BF16)
    v = v_ref[...]
    vo_ref[...] = v
    vb_ref[...] = v.astype(BF16)
    q = _rope(q_ref[...], cos, sin) * (A_HD ** -0.5 * LOG2_E)
    if stacked_q:
        tm = q.shape[0]
        lane = lax.broadcasted_iota(jnp.int32, (tm, HEAD_W), 1)
        for h in range(A_HEADS):
            qh = q[:, h * HEAD_W:(h + 1) * HEAD_W]
            qo_ref[h, 0] = jnp.where(lane < A_HD, qh, 0.0).astype(BF16)
            qo_ref[h, 1] = jnp.where(lane >= A_HD, qh, 0.0).astype(BF16)
    else:
        qo_ref[...] = q.astype(BF16)


def _rope_call(proj, cos, sin_signed, *, seq_len, tm):
    m = proj.shape[0]
    stacked = seq_len > 0
    if stacked:
        tps = seq_len // tm
        tab_map = lambda i: (i % tps, 0)
        q_shape = jax.ShapeDtypeStruct((m // seq_len, A_HEADS, 2, seq_len, HEAD_W), BF16)
        q_spec = pl.BlockSpec((None, A_HEADS, 2, tm, HEAD_W),
                              lambda i: (i // tps, 0, 0, i % tps, 0))
        k_shape = jax.ShapeDtypeStruct((m // seq_len, BRANCH_W, seq_len), F32)
        k_spec = pl.BlockSpec((None, BRANCH_W, tm), lambda i: (i // tps, 0, i % tps))
    else:
        tab_map = lambda i: (0, 0)
        q_shape = jax.ShapeDtypeStruct((m, BRANCH_W), BF16)
        q_spec = pl.BlockSpec((tm, BRANCH_W), lambda i: (i, 0))
        k_shape = jax.ShapeDtypeStruct((m, BRANCH_W), F32)
        k_spec = pl.BlockSpec((tm, BRANCH_W), lambda i: (i, 0))
    col = lambda c: pl.BlockSpec((tm, BRANCH_W), lambda i, c=c: (i, c))
    row = pl.BlockSpec((tm, BRANCH_W), lambda i: (i, 0))
    return pl.pallas_call(
        functools.partial(_rope_kernel, stacked_q=stacked),
        out_shape=(k_shape,
                   jax.ShapeDtypeStruct((m, BRANCH_W), F32),
                   jax.ShapeDtypeStruct((m, BRANCH_W), BF16),
                   jax.ShapeDtypeStruct((m, BRANCH_W), BF16),
                   q_shape),
        grid=(m // tm,),
        in_specs=[col(COL_Q), col(COL_K), col(COL_V),
                  pl.BlockSpec((tm, HEAD_W), tab_map),
                  pl.BlockSpec((tm, HEAD_W), tab_map)],
        out_specs=(k_spec, row, row, row, q_spec),
        compiler_params=_params("parallel"),
        name="rope",
    )(proj, proj, proj, cos, sin_signed)


def _rope_tables(pos):
    half = A_HD // 2
    inv = 1.0 / (ROPE_THETA ** (jnp.arange(half, dtype=F32) / half))
    ang = pos.astype(F32)[:, None] * inv[None, :]
    cos, sin = jnp.cos(ang), jnp.sin(ang)
    cos = jnp.concatenate([cos, cos], axis=-1)
    sin = jnp.concatenate([-sin, sin], axis=-1)
    return jnp.tile(cos, (1, 2)), jnp.tile(sin, (1, 2))


def _lambda(lam_ref, lam_init):
    lv = lam_ref[...]
    l1 = jnp.sum(lv[0:1] * lv[1:2], axis=-1, keepdims=True)
    l2 = jnp.sum(lv[2:3] * lv[3:4], axis=-1, keepdims=True)
    return jnp.exp(l1) - jnp.exp(l2) + lam_init


def _diff_head_out(o1, o2, lam, subg, z, lam_init):
    d = o1 - lam * o2
    y = d * lax.rsqrt(jnp.mean(d * d, axis=-1, keepdims=True) + SUBLN_EPS) * subg
    return (y * (1.0 - lam_init)) * _silu(z)


_NT = (((1,), (1,)), ((), ()))
LANES = 128


def _lane_block_sum(p):
    out = p[:, :LANES]
    for c in range(1, p.shape[1] // LANES):
        out = out + p[:, c * LANES:(c + 1) * LANES]
    return out


def _softmax_block(s, m_prev):
    m_new = jnp.maximum(m_prev, jnp.max(s, axis=-1, keepdims=True))
    alpha = jnp.exp2(m_prev - m_new)
    p = jnp.exp2(s - jnp.tile(m_new, (1, s.shape[1] // LANES)))
    return m_new, alpha, p


def _attn_p_kernel(q_ref, k_ref, v_ref, z_ref, subg_ref, lam_ref, o_ref,
                   m_ref, l_ref, acc_ref, *, tq, tk, hpb, lam_init):
    qi = pl.program_id(2)
    rows = 2 * tq

    def block(start, diagonal):
        for j in range(hpb):
            cols = slice(j * HEAD_W, (j + 1) * HEAD_W)
            q = q_ref[j].reshape(rows, HEAD_W)
            s = lax.dot_general(q, k_ref[pl.ds(start, tk), cols], _NT,
                                preferred_element_type=F32)
            v = v_ref[pl.ds(start, tk), cols]
            if diagonal:
                r = lax.broadcasted_iota(jnp.int32, s.shape, 0)
                c = lax.broadcasted_iota(jnp.int32, s.shape, 1)
                s = jnp.where(start + c <= qi * tq + r % tq, s, NEG)
                m_new = jnp.broadcast_to(jnp.max(s, axis=-1, keepdims=True), (rows, LANES))
                p = jnp.exp2(s - jnp.tile(m_new, (1, tk // LANES)))
                l_ref[j] = _lane_block_sum(p)
                acc_ref[j] = jnp.dot(p.astype(BF16), v, preferred_element_type=F32)
            else:
                m_new, alpha, p = _softmax_block(s, m_ref[j])
                l_ref[j] = alpha * l_ref[j] + _lane_block_sum(p)
                acc_ref[j] = alpha * acc_ref[j] + jnp.dot(
                    p.astype(BF16), v, preferred_element_type=F32)
            m_ref[j] = m_new

    n_full = (qi * tq) // tk
    block(pl.multiple_of(n_full * tk, tk), True)

    def body(ki, carry):
        block(pl.multiple_of(ki * tk, tk), False)
        return carry

    lax.fori_loop(0, n_full, body, 0)

    lam = _lambda(lam_ref, lam_init)
    for j in range(hpb):
        cols = slice(j * HEAD_W, (j + 1) * HEAD_W)
        o = acc_ref[j] / jnp.sum(l_ref[j], axis=-1, keepdims=True)
        out = _diff_head_out(o[:tq], o[tq:], lam, subg_ref[...], z_ref[:, cols], lam_init)
        o_ref[:, cols] = out.astype(o_ref.dtype)


def _attn_prompt(q_st, k_bf, v_bf, proj, subg, lam_vecs, lam_init, *, batch, seq_len,
                 tq=256, tk=256, hpb=8):
    nq = seq_len // tq
    gw = hpb * HEAD_W
    zcol = COL_Z * (BRANCH_W // gw)
    kv_spec = pl.BlockSpec((seq_len, gw), lambda b, g, qi: (b, g))
    return pl.pallas_call(
        functools.partial(_attn_p_kernel, tq=tq, tk=tk, hpb=hpb, lam_init=lam_init),
        out_shape=jax.ShapeDtypeStruct((batch * seq_len, BRANCH_W), BF16),
        grid=(batch, A_HEADS // hpb, nq),
        in_specs=[pl.BlockSpec((None, hpb, 2, tq, HEAD_W), lambda b, g, qi: (b, g, 0, qi, 0)),
                  kv_spec, kv_spec,
                  pl.BlockSpec((tq, gw), lambda b, g, qi: (b * nq + qi, zcol + g)),
                  pl.BlockSpec((1, HEAD_W), lambda b, g, qi: (0, 0)),
                  pl.BlockSpec((4, A_HD), lambda b, g, qi: (0, 0))],
        out_specs=pl.BlockSpec((tq, gw), lambda b, g, qi: (b * nq + qi, g)),
        scratch_shapes=[pltpu.VMEM((hpb, 2 * tq, LANES), F32),
                        pltpu.VMEM((hpb, 2 * tq, LANES), F32),
                        pltpu.VMEM((hpb, 2 * tq, HEAD_W), F32)],
        compiler_params=_params("parallel", "parallel", "arbitrary"),
        name="attn_prompt",
    )(q_st, k_bf, v_bf, proj, subg, lam_vecs)


def _attn_s_kernel(pt_ref, q_ref, kn_ref, vn_ref, *rest, t, pps, lam_init):
    del pt_ref
    kc_refs, vc_refs = rest[:pps], rest[pps:2 * pps]
    (z_ref, subg_ref, lam_ref, o_ref,
     qbd_ref, knp_ref, vnp_ref, m_ref, l_ref, acc_ref) = rest[2 * pps:]
    p_idx = pl.program_id(1)
    hrows = 2 * t

    def stats(s):
        m_new, alpha, p = _softmax_block(s, m_ref[...])
        l_ref[...] = alpha * l_ref[...] + _lane_block_sum(p)
        m_ref[...] = m_new
        return alpha, p.astype(BF16)

    @pl.when((p_idx == 0) & (pl.program_id(0) == 0))
    def _():
        knp_ref[...] = jnp.zeros(knp_ref.shape, BF16)
        vnp_ref[...] = jnp.zeros(vnp_ref.shape, BF16)

    @pl.when(p_idx == 0)
    def _():
        q = q_ref[...]
        lane = lax.broadcasted_iota(jnp.int32, q.shape, 1)
        for hc in range(2 * A_HEADS):
            keep = (lane >= hc * A_HD) & (lane < (hc + 1) * A_HD)
            qbd_ref[hc * t:(hc + 1) * t, :] = jnp.where(keep, q, jnp.zeros_like(q))
        m_ref[...] = jnp.full(m_ref.shape, NEG, F32)
        l_ref[...] = jnp.zeros(l_ref.shape, F32)
        knp_ref[0:t, :] = kn_ref[...]
        vnp_ref[0:t, :] = vn_ref[...]
        s = lax.dot_general(qbd_ref[...], knp_ref[...], _NT, preferred_element_type=F32)
        r = lax.broadcasted_iota(jnp.int32, s.shape, 0)
        c = lax.broadcasted_iota(jnp.int32, s.shape, 1)
        s = jnp.where(c <= (r % t), s, NEG)
        _, pb = stats(s)
        acc_ref[...] = jnp.concatenate(
            [jnp.dot(pb[h * hrows:(h + 1) * hrows], vnp_ref[:, h * HEAD_W:(h + 1) * HEAD_W],
                     preferred_element_type=F32) for h in range(A_HEADS)], axis=0)

    kcat = jnp.concatenate(
        [r[...].reshape(BRANCH_W, PAGE_SIZE).astype(BF16) for r in kc_refs], axis=1)
    s = jnp.dot(qbd_ref[...], kcat, preferred_element_type=F32)
    alpha, pb = stats(s)
    parts = []
    for h in range(A_HEADS):
        vh = jnp.concatenate(
            [r[pl.ds(h, PAGE_SIZE, stride=A_HEADS), :].astype(BF16) for r in vc_refs], axis=0)
        parts.append(jnp.dot(pb[h * hrows:(h + 1) * hrows], vh, preferred_element_type=F32))
    acc_ref[...] = alpha * acc_ref[...] + jnp.concatenate(parts, axis=0)

    @pl.when(p_idx == pl.num_programs(1) - 1)
    def _():
        o = acc_ref[...] / jnp.sum(l_ref[...], axis=-1, keepdims=True)
        lam = _lambda(lam_ref, lam_init)
        subg = subg_ref[...]
        for h in range(A_HEADS):
            o1 = o[(2 * h) * t:(2 * h + 1) * t]
            o2 = o[(2 * h + 1) * t:(2 * h + 2) * t]
            zh = z_ref[:, h * HEAD_W:(h + 1) * HEAD_W]
            out = _diff_head_out(o1, o2, lam, subg, zh, lam_init)
            o_ref[:, h * HEAD_W:(h + 1) * HEAD_W] = out.astype(o_ref.dtype)


def _attn_sample(q_bf, k_bf, v_bf, cache_k, cache_v, layer, page_table, proj, subg, lam_vecs,
                 lam_init, *, n_seq, t, pps=16):
    n_pages = page_table.shape[1]
    rows = 2 * A_HEADS * t
    pt_flat = page_table.reshape(-1)
    kc = jnp.transpose(cache_k, (0, 1, 3, 4, 2))
    vc = cache_v.reshape(cache_v.shape[0], cache_v.shape[1], PAGE_SIZE * A_HEADS, HEAD_W)
    seq_spec = pl.BlockSpec((t, BRANCH_W), lambda b, p, pt: (b, 0))

    def page(i, ndim=3):
        return lambda b, p, pt: (layer, pt[b * n_pages + p * pps + i]) + (0,) * ndim

    k_specs = [pl.BlockSpec((None, None, 2 * A_HEADS, A_HD, PAGE_SIZE), page(i))
               for i in range(pps)]
    v_specs = [pl.BlockSpec((None, None, PAGE_SIZE * A_HEADS, HEAD_W), page(i, 2))
               for i in range(pps)]
    grid_spec = pltpu.PrefetchScalarGridSpec(
        num_scalar_prefetch=1,
        grid=(n_seq, n_pages // pps),
        in_specs=[seq_spec, seq_spec, seq_spec] + k_specs + v_specs + [
            pl.BlockSpec((t, BRANCH_W), lambda b, p, pt: (b, COL_Z)),
            pl.BlockSpec((1, HEAD_W), lambda b, p, pt: (0, 0)),
            pl.BlockSpec((4, A_HD), lambda b, p, pt: (0, 0))],
        out_specs=pl.BlockSpec((t, BRANCH_W), lambda b, p, pt: (b, 0)),
        scratch_shapes=[pltpu.VMEM((rows, BRANCH_W), BF16),
                        pltpu.VMEM((PAGE_SIZE, BRANCH_W), BF16),
                        pltpu.VMEM((PAGE_SIZE, BRANCH_W), BF16),
                        pltpu.VMEM((rows, LANES), F32), pltpu.VMEM((rows, LANES), F32),
                        pltpu.VMEM((rows, HEAD_W), F32)],
    )
    return pl.pallas_call(
        functools.partial(_attn_s_kernel, t=t, pps=pps, lam_init=lam_init),
        out_shape=jax.ShapeDtypeStruct((n_seq * t, BRANCH_W), BF16),
        grid_spec=grid_spec,
        compiler_params=_params("arbitrary", "arbitrary"),
        name="attn_sample",
    )(pt_flat, q_bf, k_bf, v_bf, *([kc] * pps), *([vc] * pps), proj, subg, lam_vecs)


def _window_sum(e, log2_win):
    s = e
    for k in range(log2_win):
        s = s + pltpu.roll(s, 1 << k, 0)
    return s


def _pool_groups(ext, pos, w_ref, scale_ref, z, o_ref, pick):
    for gi, win in enumerate(POOL_WINDOWS):
        c0, c1 = gi * POOL_GW, (gi + 1) * POOL_GW
        e = ext[:, c0:c1]
        cur = pick(e)
        tot = pick(_window_sum(e, gi + 1))
        cnt = jnp.minimum(pos + 1, win).astype(F32)
        mixed = tot / cnt - cur
        out = jnp.dot(mixed.astype(BF16), w_ref[gi].astype(BF16), preferred_element_type=F32)
        out = out * scale_ref[:, c0:c1]
        o_ref[:, c0:c1] = (out * _silu(z[:, c0:c1])).astype(o_ref.dtype)


def _pool_p_kernel(u_ref, prev_ref, z_ref, w_ref, scale_ref, o_ref, *, tm, tps):
    i = pl.program_id(0)
    first = (i % tps) == 0
    prev = jnp.where(first, 0.0, prev_ref[...])
    ext = jnp.concatenate([prev, u_ref[...]], axis=0)
    pos = (i % tps) * tm + lax.broadcasted_iota(jnp.int32, (tm, POOL_GW), 0)
    _pool_groups(ext, pos, w_ref, scale_ref, z_ref[...], o_ref, lambda x: x[POOL_PAD:])


def _pool_s_kernel(u_ref, buf_ref, z_ref, w_ref, scale_ref, o_ref, *, nb, t, past_len):
    u3 = u_ref[...].reshape(nb, t, BRANCH_W)
    ext = jnp.concatenate([buf_ref[...], u3], axis=1)
    ext = ext.reshape(nb * (POOL_PAD + t), BRANCH_W)
    pos = past_len + lax.broadcasted_iota(jnp.int32, (nb * t, POOL_GW), 0) % t

    def pick(x):
        x3 = x.reshape(nb, POOL_PAD + t, x.shape[-1])
        return x3[:, POOL_PAD:].reshape(nb * t, x.shape[-1])

    _pool_groups(ext, pos, w_ref, scale_ref, z_ref[...], o_ref, pick)


def _pool_prompt(proj, pool_w, pool_scale, layer, *, seq_len, tm=512):
    m = proj.shape[0]
    tps = seq_len // tm
    per = tm // POOL_PAD
    return pl.pallas_call(
        functools.partial(_pool_p_kernel, tm=tm, tps=tps),
        out_shape=jax.ShapeDtypeStruct((m, BRANCH_W), BF16),
        grid=(m // tm,),
        in_specs=[pl.BlockSpec((tm, BRANCH_W), lambda i: (i, COL_U)),
                  pl.BlockSpec((POOL_PAD, BRANCH_W),
                               lambda i: (jnp.maximum(i * per - 1, 0), COL_U)),
                  pl.BlockSpec((tm, BRANCH_W), lambda i: (i, COL_Z + 1)),
                  pl.BlockSpec((None,) + pool_w.shape[1:], lambda i: (layer, 0, 0, 0)),
                  pl.BlockSpec((1, BRANCH_W), lambda i: (0, 0))],
        out_specs=pl.BlockSpec((tm, BRANCH_W), lambda i: (i, 0)),
        compiler_params=_params("parallel"),
        name="pool_prompt",
    )(proj, proj, proj, pool_w, pool_scale.reshape(1, BRANCH_W))


def _pool_sample(proj, buf_padded, pool_w, pool_scale, layer, *, t, past_len, nb=32):
    m = proj.shape[0]
    return pl.pallas_call(
        functools.partial(_pool_s_kernel, nb=nb, t=t, past_len=past_len),
        out_shape=jax.ShapeDtypeStruct((m, BRANCH_W), BF16),
        grid=(m // (nb * t),),
        in_specs=[pl.BlockSpec((nb * t, BRANCH_W), lambda i: (i, COL_U)),
                  pl.BlockSpec((nb, POOL_PAD, BRANCH_W), lambda i: (i, 0, 0)),
                  pl.BlockSpec((nb * t, BRANCH_W), lambda i: (i, COL_Z + 1)),
                  pl.BlockSpec((None,) + pool_w.shape[1:], lambda i: (layer, 0, 0, 0)),
                  pl.BlockSpec((1, BRANCH_W), lambda i: (0, 0))],
        out_specs=pl.BlockSpec((nb * t, BRANCH_W), lambda i: (i, 0)),
        compiler_params=_params("parallel"),
        name="pool_sample",
    )(proj, buf_padded, proj, pool_w, pool_scale.reshape(1, BRANCH_W))


def _conv_taps(ext, pick, cw_ref, cb_ref):
    out = cb_ref[...] + pick(ext) * cw_ref[CONV_W - 1:CONV_W, :]
    for k in range(CONV_W - 1):
        shift = CONV_W - 1 - k
        out = out + pick(pltpu.roll(ext, shift, 0)) * cw_ref[k:k + 1, :]
    return out


def _block_diag_gate(xb, w_ref, b_ref):
    parts = [jnp.dot(xb[:, n * LRU_BW:(n + 1) * LRU_BW], w_ref[n].astype(BF16),
                     preferred_element_type=F32) for n in range(LRU_BLOCKS)]
    return jax.nn.sigmoid(jnp.concatenate(parts, axis=-1) + b_ref[...])


def _lru_coeffs(xc, wa_ref, ba_ref, wx_ref, bx_ref, lam_ref):
    xb = xc.astype(BF16)
    r = _block_diag_gate(xb, wa_ref, ba_ref)
    i = _block_diag_gate(xb, wx_ref, bx_ref)
    nl = -lam_ref[...]
    softplus = jnp.maximum(nl, 0.0) + jnp.log1p(jnp.exp(-jnp.abs(nl)))
    log_a = r * (-LRU_C * softplus)
    a = jnp.exp(log_a)
    b = jnp.sqrt(1.0 - a * a) * (i * xc)
    return a, b


def _group_scan(a, b):
    shape = a.shape
    grouped = (shape[0] // SUBLANES, SUBLANES, shape[1])
    a = a.reshape(grouped)
    b = b.reshape(grouped)
    row = lax.broadcasted_iota(jnp.int32, grouped, 1)
    d = 1
    while d < SUBLANES:
        valid = row >= d
        a_sh = pltpu.roll(a, d, 1)
        b_sh = pltpu.roll(b, d, 1)
        b = jnp.where(valid, a * b_sh + b, b)
        a = jnp.where(valid, a * a_sh, a)
        d *= 2
    return a.reshape(shape), b.reshape(shape)


def _lru_p_kernel(x_ref, prev_ref, z_ref, cw_ref, cb_ref, wa_ref, ba_ref, wx_ref, bx_ref,
                  lam_ref, o_ref, hl_ref, a_s, b_s, h_s, *, tm, tps):
    i = pl.program_id(0)
    first = (i % tps) == 0

    @pl.when(first)
    def _():
        h_s[...] = jnp.zeros(h_s.shape, F32)

    prev = jnp.where(first, 0.0, prev_ref[...])
    ext = jnp.concatenate([prev, x_ref[...]], axis=0)
    xc = _conv_taps(ext, lambda x: x[CONV_PAD:], cw_ref, cb_ref)
    a, b = _lru_coeffs(xc, wa_ref, ba_ref, wx_ref, bx_ref, lam_ref)
    a, b = _group_scan(a, b)
    a_s[...] = a
    b_s[...] = b

    def body(g, h):
        r0 = pl.multiple_of(g * SUBLANES, SUBLANES)
        hs = a_s[pl.ds(r0, SUBLANES), :] * h + b_s[pl.ds(r0, SUBLANES), :]
        b_s[pl.ds(r0, SUBLANES), :] = hs
        return hs[SUBLANES - 1:SUBLANES, :]

    h = lax.fori_loop(0, tm // SUBLANES, body, h_s[...])
    h_s[...] = h
    hl_ref[...] = h
    o_ref[...] = (b_s[...] * _silu(z_ref[...])).astype(o_ref.dtype)


def _lru_s_kernel(x_ref, buf_ref, h0_ref, z_ref, cw_ref, cb_ref, wa_ref, ba_ref, wx_ref, bx_ref,
                  lam_ref, o_ref, hs_ref, *, nb, t):
    x3 = x_ref[...].reshape(nb, t, BRANCH_W)
    ext = jnp.concatenate([buf_ref[...], x3], axis=1).reshape(nb * (CONV_PAD + t), BRANCH_W)

    def pick(x):
        return x.reshape(nb, CONV_PAD + t, BRANCH_W)[:, CONV_PAD:].reshape(nb * t, BRANCH_W)

    xc = _conv_taps(ext, pick, cw_ref, cb_ref)
    a, b = _lru_coeffs(xc, wa_ref, ba_ref, wx_ref, bx_ref, lam_ref)
    a, b = _group_scan(a, b)
    hs = a * h0_ref[...] + b
    hs_ref[...] = hs
    o_ref[...] = (hs * _silu(z_ref[...])).astype(o_ref.dtype)


def _lru_weight_specs():
    full = lambda shape: pl.BlockSpec(shape, lambda *a: (0,) * len(shape))
    return [full((CONV_W, BRANCH_W)), full((1, BRANCH_W)),
            full((LRU_BLOCKS, LRU_BW, LRU_BW)), full((1, BRANCH_W)),
            full((LRU_BLOCKS, LRU_BW, LRU_BW)), full((1, BRANCH_W)),
            full((1, BRANCH_W))]


def _lru_weights(conv_w, conv_b, wa, ba, wx, bx, lam):
    r = lambda v: v.reshape(1, BRANCH_W)
    return (conv_w, r(conv_b), wa, r(ba), wx, r(bx), r(lam))


def _lru_prompt(proj, weights, *, batch, seq_len, tm=512):
    m = proj.shape[0]
    tps = seq_len // tm
    per = tm // CONV_PAD
    return pl.pallas_call(
        functools.partial(_lru_p_kernel, tm=tm, tps=tps),
        out_shape=(jax.ShapeDtypeStruct((m, BRANCH_W), BF16),
                   jax.ShapeDtypeStruct((batch, 1, BRANCH_W), F32)),
        grid=(m // tm,),
        in_specs=[pl.BlockSpec((tm, BRANCH_W), lambda i: (i, COL_X)),
                  pl.BlockSpec((CONV_PAD, BRANCH_W),
                               lambda i: (jnp.maximum(i * per - 1, 0), COL_X)),
                  pl.BlockSpec((tm, BRANCH_W), lambda i: (i, COL_Z + 2))]
        + _lru_weight_specs(),
        out_specs=(pl.BlockSpec((tm, BRANCH_W), lambda i: (i, 0)),
                   pl.BlockSpec((None, 1, BRANCH_W), lambda i: (i // tps, 0, 0))),
        scratch_shapes=[pltpu.VMEM((tm, BRANCH_W), F32), pltpu.VMEM((tm, BRANCH_W), F32),
                        pltpu.VMEM((1, BRANCH_W), F32)],
        compiler_params=_params("arbitrary"),
        name="lru_prompt",
    )(proj, proj, proj, *weights)


def _lru_sample(proj, buf_padded, h0_rows, weights, *, t, nb=32):
    m = proj.shape[0]
    rows = nb * t
    return pl.pallas_call(
        functools.partial(_lru_s_kernel, nb=nb, t=t),
        out_shape=(jax.ShapeDtypeStruct((m, BRANCH_W), BF16),
                   jax.ShapeDtypeStruct((m, BRANCH_W), F32)),
        grid=(m // rows,),
        in_specs=[pl.BlockSpec((rows, BRANCH_W), lambda i: (i, COL_X)),
                  pl.BlockSpec((nb, CONV_PAD, BRANCH_W), lambda i: (i, 0, 0)),
                  pl.BlockSpec((rows, BRANCH_W), lambda i: (i, 0)),
                  pl.BlockSpec((rows, BRANCH_W), lambda i: (i, COL_Z + 2))]
        + _lru_weight_specs(),
        out_specs=(pl.BlockSpec((rows, BRANCH_W), lambda i: (i, 0)),
                   pl.BlockSpec((rows, BRANCH_W), lambda i: (i, 0))),
        compiler_params=_params("parallel"),
        name="lru_sample",
    )(proj, buf_padded, h0_rows, proj, *weights)


def _mem_attn_kernel(q_ref, k_ref, v_ref, z_ref, o_ref):
    q = q_ref[...].astype(BF16)
    k = k_ref[...].astype(BF16)
    v = v_ref[...].astype(BF16)
    z = z_ref[...]
    for h in range(M_HEADS):
        c0, c1 = h * M_HD, (h + 1) * M_HD
        s = lax.dot_general(q[:, c0:c1], k[:, c0:c1], _NT, preferred_element_type=F32)
        s = s * (M_HD ** -0.5)
        p = jnp.exp(s - jnp.max(s, axis=-1, keepdims=True))
        den = jnp.sum(p, axis=-1, keepdims=True)
        o = jnp.dot((p / den).astype(BF16), v[:, c0:c1], preferred_element_type=F32)
        o_ref[:, c0:c1] = (o * _silu(z[:, c0:c1])).astype(o_ref.dtype)


def _mem_attn(proj, mem_k, mem_v, layer, *, n_seq, rows_per_seq, tq):
    n_mem = mem_k.shape[1] // n_seq
    nq = rows_per_seq // tq
    kv_spec = pl.BlockSpec((None, n_mem, BRANCH_W), lambda b, qi: (layer, b, 0))
    return pl.pallas_call(
        _mem_attn_kernel,
        out_shape=jax.ShapeDtypeStruct((n_seq * rows_per_seq, BRANCH_W), BF16),
        grid=(n_seq, nq),
        in_specs=[pl.BlockSpec((tq, BRANCH_W), lambda b, qi: (b * nq + qi, COL_QD)),
                  kv_spec, kv_spec,
                  pl.BlockSpec((tq, BRANCH_W), lambda b, qi: (b * nq + qi, COL_Z + 3))],
        out_specs=pl.BlockSpec((tq, BRANCH_W), lambda b, qi: (b * nq + qi, 0)),
        compiler_params=_params("parallel", "parallel"),
        name="mem_attn",
    )(proj, mem_k, mem_v, proj)


def _mem_attn_s_kernel(q_ref, k_ref, v_ref, z_ref, o_ref, *, nb, t):
    n_mem = k_ref.shape[1]
    rows = M_HEADS * t
    col = lax.broadcasted_iota(jnp.int32, (rows, n_mem * M_HEADS), 1)
    row = lax.broadcasted_iota(jnp.int32, (rows, n_mem * M_HEADS), 0)
    own_head = (col % M_HEADS) == (row // t)
    for b in range(nb):
        q = q_ref[b * t:(b + 1) * t, :].astype(BF16)
        qs = jnp.concatenate([q[:, h * M_HD:(h + 1) * M_HD] for h in range(M_HEADS)], axis=0)
        k2 = k_ref[b].reshape(n_mem * M_HEADS, M_HD).astype(BF16)
        v2 = v_ref[b].reshape(n_mem * M_HEADS, M_HD).astype(BF16)
        s = lax.dot_general(qs, k2, _NT, preferred_element_type=F32) * (M_HD ** -0.5)
        s = jnp.where(own_head, s, NEG)
        p = jnp.exp(s - jnp.max(s, axis=-1, keepdims=True))
        den = jnp.sum(p, axis=-1, keepdims=True)
        o = jnp.dot((p / den).astype(BF16), v2, preferred_element_type=F32)
        for h in range(M_HEADS):
            c0, c1 = h * M_HD, (h + 1) * M_HD
            zh = z_ref[b * t:(b + 1) * t, c0:c1]
            o_ref[b * t:(b + 1) * t, c0:c1] = (o[h * t:(h + 1) * t] * _silu(zh)).astype(o_ref.dtype)


def _mem_attn_sample(proj, cache_mem_k, cache_mem_v, layer, *, t, nb=8):
    _, n_seq, n_mem, _, _ = cache_mem_k.shape
    kv_spec = pl.BlockSpec((None, nb, n_mem, M_HEADS, M_HD), lambda i: (layer, i, 0, 0, 0))
    return pl.pallas_call(
        functools.partial(_mem_attn_s_kernel, nb=nb, t=t),
        out_shape=jax.ShapeDtypeStruct((n_seq * t, BRANCH_W), BF16),
        grid=(n_seq // nb,),
        in_specs=[pl.BlockSpec((nb * t, BRANCH_W), lambda i: (i, COL_QD)),
                  kv_spec, kv_spec,
                  pl.BlockSpec((nb * t, BRANCH_W), lambda i: (i, COL_Z + 3))],
        out_specs=pl.BlockSpec((nb * t, BRANCH_W), lambda i: (i, 0)),
        compiler_params=_params("parallel"),
        name="mem_attn_sample",
    )(proj, cache_mem_k, cache_mem_v, proj)


def _merge_kernel(*refs):
    h_refs, g_refs = refs[:N_BRANCH], refs[N_BRANCH:2 * N_BRANCH]
    w_ref, o_ref = refs[2 * N_BRANCH:]
    acc = None
    for n in range(N_BRANCH):
        up = jnp.dot(h_refs[n][...], w_ref[n], preferred_element_type=F32)
        term = jax.nn.sigmoid(g_refs[n][...]) * up
        acc = term if acc is None else acc + term
    o_ref[...] = acc.astype(o_ref.dtype)


def _merge(h_a, h_b, h_c, h_d, proj, w_branch_bf, layer, tm=256, tn=D_MODEL):
    m = h_a.shape[0]
    tm = min(tm, m)
    h_spec = pl.BlockSpec((tm, BRANCH_W), lambda i, j: (i, 0))
    gate_col0 = COL_G * BRANCH_W // tn
    g_specs = [pl.BlockSpec((tm, tn), lambda i, j, n=n: (i, gate_col0 + n * (D_MODEL // tn) + j))
               for n in range(N_BRANCH)]
    return pl.pallas_call(
        _merge_kernel,
        out_shape=jax.ShapeDtypeStruct((m, D_MODEL), BF16),
        grid=(m // tm, D_MODEL // tn),
        in_specs=[h_spec] * N_BRANCH + g_specs + [
            pl.BlockSpec((None, N_BRANCH, BRANCH_W, tn), lambda i, j: (layer, 0, 0, j),
                         pipeline_mode=pl.Buffered(1))],
        out_specs=pl.BlockSpec((tm, tn), lambda i, j: (i, j)),
        compiler_params=_params("parallel", "parallel"),
        name="merge",
    )(h_a, h_b, h_c, h_d, proj, proj, proj, proj, w_branch_bf)


def _layer_dense_in(x, ln_g, w_in, layer):
    xn = _rmsnorm(x, ln_g, EPS, BF16)
    return _matmul(xn, w_in, layer, tn=1536)


def _layer_dense_out(x, h_a, h_b, h_c, h_d, proj, w_branch_bf, w_out, layer):
    merged = _merge(h_a, h_b, h_c, h_d, proj, w_branch_bf, layer)
    return _matmul(merged, w_out, layer, res=x)


def kernel(x_prompt, x_sample, cache_k, cache_v, cache_mem_k, cache_mem_v, state_pool, state_conv, state_h, page_table, mem_prompt, ln_g, w_in, lam_q1, lam_k1, lam_q2, lam_k2, attn_sub_g, pool_w, pool_scale, conv_w, conv_b, lru_wa, lru_ba, lru_wx, lru_bx, lru_lambda, mem_g, w_mem_k, w_mem_v, w_branch, w_out, final_g):
    batch, seq_len, _ = x_prompt.shape
    n_seq, t, _ = x_sample.shape
    depth = w_in.shape[0]
    n_mem = mem_prompt.shape[1]
    past_len = page_table.shape[1] * PAGE_SIZE
    rope_tm = 256

    cos_p, sin_p = _rope_tables(jnp.arange(seq_len, dtype=jnp.int32))
    cos_s, sin_s = _rope_tables(past_len + jnp.arange(t, dtype=jnp.int32))
    cos_s = jnp.tile(cos_s, (rope_tm // t, 1))
    sin_s = jnp.tile(sin_s, (rope_tm // t, 1))

    xp = x_prompt.reshape(batch * seq_len, D_MODEL)
    xs = x_sample.reshape(n_seq * t, D_MODEL)
    mem_rows = mem_prompt.reshape(batch * n_mem, D_MODEL)

    w_branch_bf = w_branch.astype(BF16)

    outs = {name: [] for name in ("kp", "vp", "ks", "vs", "mk", "mv", "pp", "ps", "cp", "cs", "hp", "hs")}
    for l in range(depth):
        lam_init = 0.8 - 0.6 * math.exp(-0.3 * l)
        lam_vecs = jnp.stack([lam_q1[l], lam_k1[l], lam_q2[l], lam_k2[l]])
        subg = attn_sub_g[l].reshape(1, HEAD_W)
        lru_w = _lru_weights(conv_w[l], conv_b[l], lru_wa[l], lru_ba[l], lru_wx[l], lru_bx[l],
                             lru_lambda[l])

        mn = _rmsnorm(mem_rows, mem_g[l], EPS, BF16)
        mk = _matmul(mn, w_mem_k, l)
        mv = _matmul(mn, w_mem_v, l)
        proj = _layer_dense_in(xp, ln_g[l], w_in, l)
        k_f, v_f, k_bf, v_bf, q_st = _rope_call(proj, cos_p, sin_p, seq_len=seq_len, tm=rope_tm)
        h_a = _attn_prompt(q_st, k_bf, v_bf, proj, subg, lam_vecs, lam_init,
                           batch=batch, seq_len=seq_len)
        h_b = _pool_prompt(proj, pool_w, pool_scale[l], l, seq_len=seq_len)
        h_c, h_last = _lru_prompt(proj, lru_w, batch=batch, seq_len=seq_len)
        h_d = _mem_attn(proj, mk[None], mv[None], 0, n_seq=batch, rows_per_seq=seq_len, tq=1024)
        proj3 = proj.reshape(batch, seq_len, -1)
        outs["pp"].append(proj3[:, seq_len - POOL_BUF:, COL_U * BRANCH_W:(COL_U + 1) * BRANCH_W])
        outs["cp"].append(proj3[:, seq_len - (CONV_W - 1):, COL_X * BRANCH_W:(COL_X + 1) * BRANCH_W])
        xp = _layer_dense_out(xp, h_a, h_b, h_c, h_d, proj, w_branch_bf, w_out, l)
        outs["kp"].append(k_f.reshape(batch, 2 * A_HEADS, A_HD, seq_len))
        outs["vp"].append(v_f.reshape(batch, seq_len, A_HEADS, HEAD_W))
        outs["mk"].append(mk.reshape(batch, n_mem, M_HEADS, M_HD))
        outs["mv"].append(mv.reshape(batch, n_mem, M_HEADS, M_HD))
        outs["hp"].append(h_last.reshape(batch, BRANCH_W))

        proj = _layer_dense_in(xs, ln_g[l], w_in, l)
        k_f, v_f, k_bf, v_bf, q_bf = _rope_call(proj, cos_s, sin_s, seq_len=0, tm=rope_tm)
        h_a = _attn_sample(q_bf, k_bf, v_bf, cache_k, cache_v, l, page_table, proj, subg,
                           lam_vecs, lam_init, n_seq=n_seq, t=t)
        pool_buf = jnp.pad(state_pool[l], ((0, 0), (POOL_PAD - POOL_BUF, 0), (0, 0)))
        h_b = _pool_sample(proj, pool_buf, pool_w, pool_scale[l], l, t=t, past_len=past_len)
        conv_buf = jnp.pad(state_conv[l], ((0, 0), (CONV_PAD - (CONV_W - 1), 0), (0, 0)))
        h0_rows = jnp.repeat(state_h[l], t, axis=0)
        h_c, hs_rows = _lru_sample(proj, conv_buf, h0_rows, lru_w, t=t)
        h_d = _mem_attn_sample(proj, cache_mem_k, cache_mem_v, l, t=t)
        proj3 = proj.reshape(n_seq, t, -1)
        u_s = proj3[:, :, COL_U * BRANCH_W:(COL_U + 1) * BRANCH_W]
        x_s = proj3[:, :, COL_X * BRANCH_W:(COL_X + 1) * BRANCH_W]
        outs["ps"].append(jnp.concatenate([state_pool[l], u_s], axis=1)[:, -POOL_BUF:])
        outs["cs"].append(jnp.concatenate([state_conv[l], x_s], axis=1)[:, -(CONV_W - 1):])
        xs = _layer_dense_out(xs, h_a, h_b, h_c, h_d, proj, w_branch_bf, w_out, l)
        outs["ks"].append(k_f.reshape(n_seq, t, 2 * A_HEADS, A_HD))
        outs["vs"].append(v_f.reshape(n_seq, t, A_HEADS, HEAD_W))
        outs["hs"].append(hs_rows.reshape(n_seq, t, BRANCH_W)[:, t - 1])

    y_prompt = _rmsnorm(xp, final_g, EPS, F32).reshape(batch, seq_len, D_MODEL)
    y_sample = _rmsnorm(xs, final_g, EPS, F32).reshape(n_seq, t, D_MODEL)
    st = lambda name: jnp.stack(outs[name])
    k_prompt = jnp.transpose(st("kp"), (0, 1, 4, 2, 3))
    return (y_prompt, y_sample, k_prompt, st("vp"), st("ks"), st("vs"), st("mk"), st("mv"),
            st("pp"), st("ps"), st("cp"), st("cs"), st("hp"), st("hs"))
```

```python
import functools
import math

import jax
import jax.numpy as jnp
from jax import lax
from jax.experimental import pallas as pl
from jax.experimental.pallas import tpu as pltpu

F32 = jnp.float32
BF16 = jnp.bfloat16

D_MODEL = 2048
BRANCH_W = 1024
A_HEADS = 8
A_HD = 64
HEAD_W = 2 * A_HD
ROPE_THETA = 10000.0
SUBLN_EPS = 1e-5
POOL_WINDOWS = (2, 4, 8, 16)
POOL_GW = BRANCH_W // len(POOL_WINDOWS)
POOL_BUF = max(POOL_WINDOWS) - 1
POOL_PAD = 16
LRU_BLOCKS = 8
LRU_BW = BRANCH_W // LRU_BLOCKS
CONV_W = 4
CONV_PAD = 8
LRU_C = 8.0
M_HEADS = 4
M_HD = BRANCH_W // M_HEADS
N_BRANCH = 4
PAGE_SIZE = 128
EPS = 1e-6
NEG = -1e30
SUBLANES = 8
LOG2_E = math.log2(math.e)

COL_Q, COL_K, COL_V, COL_U, COL_X, COL_QD = 0, 1, 2, 3, 4, 5
COL_Z = 6
COL_G = 10

VMEM_LIMIT = 56 * 1024 * 1024


def _params(*sem):
    return pltpu.CompilerParams(dimension_semantics=sem, vmem_limit_bytes=VMEM_LIMIT)


def _silu(z):
    return z * jax.nn.sigmoid(z)


def _rmsnorm_kernel(x_ref, g_ref, o_ref, *, eps):
    x = x_ref[...]
    y = x * lax.rsqrt(jnp.mean(x * x, axis=-1, keepdims=True) + eps)
    o_ref[...] = (y * g_ref[...]).astype(o_ref.dtype)


def _rmsnorm(x, g, eps, out_dtype, tm=512):
    m, d = x.shape
    return pl.pallas_call(
        functools.partial(_rmsnorm_kernel, eps=eps),
        out_shape=jax.ShapeDtypeStruct((m, d), out_dtype),
        grid=(m // tm,),
        in_specs=[pl.BlockSpec((tm, d), lambda i: (i, 0)),
                  pl.BlockSpec((1, d), lambda i: (0, 0))],
        out_specs=pl.BlockSpec((tm, d), lambda i: (i, 0)),
        compiler_params=_params("parallel"),
        name="rmsnorm",
    )(x, g.reshape(1, d))


def _mm_kernel(a_ref, w_ref, *rest, has_res):
    if has_res:
        r_ref, o_ref, wb_ref = rest
    else:
        o_ref, wb_ref = rest

    @pl.when(pl.program_id(1) == 0)
    def _():
        wb_ref[...] = w_ref[...].astype(BF16)

    acc = jnp.dot(a_ref[...], wb_ref[...], preferred_element_type=F32)
    if has_res:
        acc = r_ref[...] + acc
    o_ref[...] = acc


def _matmul(a, w, layer, res=None, tm=1024, tn=1024):
    m, k = a.shape
    n = w.shape[2]
    tm = min(tm, m)
    in_specs = [pl.BlockSpec((tm, k), lambda j, i: (i, 0)),
                pl.BlockSpec((None, k, tn), lambda j, i: (layer, 0, j))]
    args = [a, w]
    if res is not None:
        in_specs.append(pl.BlockSpec((tm, tn), lambda j, i: (i, j)))
        args.append(res)
    return pl.pallas_call(
        functools.partial(_mm_kernel, has_res=res is not None),
        out_shape=jax.ShapeDtypeStruct((m, n), F32),
        grid=(n // tn, m // tm),
        in_specs=in_specs,
        out_specs=pl.BlockSpec((tm, tn), lambda j, i: (i, j)),
        scratch_shapes=[pltpu.VMEM((k, tn), BF16)],
        compiler_params=_params("arbitrary", "arbitrary"),
        name="matmul",
    )(*args)


def _rope(x, cos, sin_signed):
    lane = lax.broadcasted_iota(jnp.int32, cos.shape, 1)
    first_half = (lane % A_HD) < (A_HD // 2)
    slabs = []
    for h in range(x.shape[1] // HEAD_W):
        xh = x[:, h * HEAD_W:(h + 1) * HEAD_W]
        partner = jnp.where(first_half,
                            pltpu.roll(xh, HEAD_W - A_HD // 2, 1),
                            pltpu.roll(xh, A_HD // 2, 1))
        slabs.append(xh * cos + partner * sin_signed)
    return jnp.concatenate(slabs, axis=1)


def _rope_kernel(q_ref, k_ref, v_ref, cos_ref, sin_ref,
                 ko_ref, vo_ref, kb_ref, vb_ref, qo_ref, *, stacked_q):
    cos = cos_ref[...]
    sin = sin_ref[...]
    k = _rope(k_ref[...], cos, sin)
    ko_ref[...] = k.T if stacked_q else k
    kb_ref[...] = k.astype(BF16)
    v = v_ref[...]
    vo_ref[...] = v
    vb_ref[...] = v.astype(BF16)
    q = _rope(q_ref[...], cos, sin) * (A_HD ** -0.5 * LOG2_E)
    if stacked_q:
        tm = q.shape[0]
        lane = lax.broadcasted_iota(jnp.int32, (tm, HEAD_W), 1)
        for h in range(A_HEADS):
            qh = q[:, h * HEAD_W:(h + 1) * HEAD_W]
            qo_ref[h, 0] = jnp.where(lane < A_HD, qh, 0.0).astype(BF16)
            qo_ref[h, 1] = jnp.where(lane >= A_HD, qh, 0.0).astype(BF16)
    else:
        qo_ref[...] = q.astype(BF16)


def _rope_call(proj, cos, sin_signed, *, seq_len, tm):
    m = proj.shape[0]
    stacked = seq_len > 0
    if stacked:
        tps = seq_len // tm
        tab_map = lambda i: (i % tps, 0)
        q_shape = jax.ShapeDtypeStruct((m // seq_len, A_HEADS, 2, seq_len, HEAD_W), BF16)
        q_spec = pl.BlockSpec((None, A_HEADS, 2, tm, HEAD_W),
                              lambda i: (i // tps, 0, 0, i % tps, 0))
        k_shape = jax.ShapeDtypeStruct((m // seq_len, BRANCH_W, seq_len), F32)
        k_spec = pl.BlockSpec((None, BRANCH_W, tm), lambda i: (i // tps, 0, i % tps))
    else:
        tab_map = lambda i: (0, 0)
        q_shape = jax.ShapeDtypeStruct((m, BRANCH_W), BF16)
        q_spec = pl.BlockSpec((tm, BRANCH_W), lambda i: (i, 0))
        k_shape = jax.ShapeDtypeStruct((m, BRANCH_W), F32)
        k_spec = pl.BlockSpec((tm, BRANCH_W), lambda i: (i, 0))
    col = lambda c: pl.BlockSpec((tm, BRANCH_W), lambda i, c=c: (i, c))
    row = pl.BlockSpec((tm, BRANCH_W), lambda i: (i, 0))
    return pl.pallas_call(
        functools.partial(_rope_kernel, stacked_q=stacked),
        out_shape=(k_shape,
                   jax.ShapeDtypeStruct((m, BRANCH_W), F32),
                   jax.ShapeDtypeStruct((m, BRANCH_W), BF16),
                   jax.ShapeDtypeStruct((m, BRANCH_W), BF16),
                   q_shape),
        grid=(m // tm,),
        in_specs=[col(COL_Q), col(COL_K), col(COL_V),
                  pl.BlockSpec((tm, HEAD_W), tab_map),
                  pl.BlockSpec((tm, HEAD_W), tab_map)],
        out_specs=(k_spec, row, row, row, q_spec),
        compiler_params=_params("parallel"),
        name="rope",
    )(proj, proj, proj, cos, sin_signed)


def _rope_tables(pos):
    half = A_HD // 2
    inv = 1.0 / (ROPE_THETA ** (jnp.arange(half, dtype=F32) / half))
    ang = pos.astype(F32)[:, None] * inv[None, :]
    cos, sin = jnp.cos(ang), jnp.sin(ang)
    cos = jnp.concatenate([cos, cos], axis=-1)
    sin = jnp.concatenate([-sin, sin], axis=-1)
    return jnp.tile(cos, (1, 2)), jnp.tile(sin, (1, 2))


def _lambda(lam_ref, lam_init):
    lv = lam_ref[...]
    l1 = jnp.sum(lv[0:1] * lv[1:2], axis=-1, keepdims=True)
    l2 = jnp.sum(lv[2:3] * lv[3:4], axis=-1, keepdims=True)
    return jnp.exp(l1) - jnp.exp(l2) + lam_init


def _diff_head_out(o1, o2, lam, subg, z, lam_init):
    d = o1 - lam * o2
    y = d * lax.rsqrt(jnp.mean(d * d, axis=-1, keepdims=True) + SUBLN_EPS) * subg
    return (y * (1.0 - lam_init)) * _silu(z)


_NT = (((1,), (1,)), ((), ()))
LANES = 128


def _lane_block_sum(p):
    out = p[:, :LANES]
    for c in range(1, p.shape[1] // LANES):
        out = out + p[:, c * LANES:(c + 1) * LANES]
    return out


def _softmax_block(s, m_prev):
    m_new = jnp.maximum(m_prev, jnp.max(s, axis=-1, keepdims=True))
    alpha = jnp.exp2(m_prev - m_new)
    p = jnp.exp2(s - jnp.tile(m_new, (1, s.shape[1] // LANES)))
    return m_new, alpha, p


def _attn_p_kernel(q_ref, k_ref, v_ref, z_ref, subg_ref, lam_ref, o_ref,
                   m_ref, l_ref, acc_ref, *, tq, tk, hpb, lam_init):
    qi = pl.program_id(2)
    rows = 2 * tq

    def block(start, diagonal):
        for j in range(hpb):
            cols = slice(j * HEAD_W, (j + 1) * HEAD_W)
            q = q_ref[j].reshape(rows, HEAD_W)
            s = lax.dot_general(q, k_ref[pl.ds(start, tk), cols], _NT,
                                preferred_element_type=F32)
            v = v_ref[pl.ds(start, tk), cols]
            if diagonal:
                r = lax.broadcasted_iota(jnp.int32, s.shape, 0)
                c = lax.broadcasted_iota(jnp.int32, s.shape, 1)
                s = jnp.where(start + c <= qi * tq + r % tq, s, NEG)
                m_new = jnp.broadcast_to(jnp.max(s, axis=-1, keepdims=True), (rows, LANES))
                p = jnp.exp2(s - jnp.tile(m_new, (1, tk // LANES)))
                l_ref[j] = _lane_block_sum(p)
                acc_ref[j] = jnp.dot(p.astype(BF16), v, preferred_element_type=F32)
            else:
                m_new, alpha, p = _softmax_block(s, m_ref[j])
                l_ref[j] = alpha * l_ref[j] + _lane_block_sum(p)
                acc_ref[j] = alpha * acc_ref[j] + jnp.dot(
                    p.astype(BF16), v, preferred_element_type=F32)
            m_ref[j] = m_new

    n_full = (qi * tq) // tk
    block(pl.multiple_of(n_full * tk, tk), True)

    def body(ki, carry):
        block(pl.multiple_of(ki * tk, tk), False)
        return carry

    lax.fori_loop(0, n_full, body, 0)

    lam = _lambda(lam_ref, lam_init)
    for j in range(hpb):
        cols = slice(j * HEAD_W, (j + 1) * HEAD_W)
        o = acc_ref[j] / jnp.sum(l_ref[j], axis=-1, keepdims=True)
        out = _diff_head_out(o[:tq], o[tq:], lam, subg_ref[...], z_ref[:, cols], lam_init)
        o_ref[:, cols] = out.astype(o_ref.dtype)


def _attn_prompt(q_st, k_bf, v_bf, proj, subg, lam_vecs, lam_init, *, batch, seq_len,
                 tq=256, tk=256, hpb=8):
    nq = seq_len // tq
    gw = hpb * HEAD_W
    zcol = COL_Z * (BRANCH_W // gw)
    kv_spec = pl.BlockSpec((seq_len, gw), lambda b, g, qi: (b, g))
    return pl.pallas_call(
        functools.partial(_attn_p_kernel, tq=tq, tk=tk, hpb=hpb, lam_init=lam_init),
        out_shape=jax.ShapeDtypeStruct((batch * seq_len, BRANCH_W), BF16),
        grid=(batch, A_HEADS // hpb, nq),
        in_specs=[pl.BlockSpec((None, hpb, 2, tq, HEAD_W), lambda b, g, qi: (b, g, 0, qi, 0)),
                  kv_spec, kv_spec,
                  pl.BlockSpec((tq, gw), lambda b, g, qi: (b * nq + qi, zcol + g)),
                  pl.BlockSpec((1, HEAD_W), lambda b, g, qi: (0, 0)),
                  pl.BlockSpec((4, A_HD), lambda b, g, qi: (0, 0))],
        out_specs=pl.BlockSpec((tq, gw), lambda b, g, qi: (b * nq + qi, g)),
        scratch_shapes=[pltpu.VMEM((hpb, 2 * tq, LANES), F32),
                        pltpu.VMEM((hpb, 2 * tq, LANES), F32),
                        pltpu.VMEM((hpb, 2 * tq, HEAD_W), F32)],
        compiler_params=_params("parallel", "parallel", "arbitrary"),
        name="attn_prompt",
    )(q_st, k_bf, v_bf, proj, subg, lam_vecs)


def _attn_s_kernel(pt_ref, q_ref, kn_ref, vn_ref, *rest, t, pps, lam_init):
    del pt_ref
    kc_refs, vc_refs = rest[:pps], rest[pps:2 * pps]
    (z_ref, subg_ref, lam_ref, o_ref,
     qbd_ref, knp_ref, vnp_ref, m_ref, l_ref, acc_ref) = rest[2 * pps:]
    p_idx = pl.program_id(1)
    hrows = 2 * t

    def stats(s):
        m_new, alpha, p = _softmax_block(s, m_ref[...])
        l_ref[...] = alpha * l_ref[...] + _lane_block_sum(p)
        m_ref[...] = m_new
        return alpha, p.astype(BF16)

    @pl.when((p_idx == 0) & (pl.program_id(0) == 0))
    def _():
        knp_ref[...] = jnp.zeros(knp_ref.shape, BF16)
        vnp_ref[...] = jnp.zeros(vnp_ref.shape, BF16)

    @pl.when(p_idx == 0)
    def _():
        q = q_ref[...]
        lane = lax.broadcasted_iota(jnp.int32, q.shape, 1)
        for hc in range(2 * A_HEADS):
            keep = (lane >= hc * A_HD) & (lane < (hc + 1) * A_HD)
            qbd_ref[hc * t:(hc + 1) * t, :] = jnp.where(keep, q, jnp.zeros_like(q))
        m_ref[...] = jnp.full(m_ref.shape, NEG, F32)
        l_ref[...] = jnp.zeros(l_ref.shape, F32)
        knp_ref[0:t, :] = kn_ref[...]
        vnp_ref[0:t, :] = vn_ref[...]
        s = lax.dot_general(qbd_ref[...], knp_ref[...], _NT, preferred_element_type=F32)
        r = lax.broadcasted_iota(jnp.int32, s.shape, 0)
        c = lax.broadcasted_iota(jnp.int32, s.shape, 1)
        s = jnp.where(c <= (r % t), s, NEG)
        _, pb = stats(s)
        acc_ref[...] = jnp.concatenate(
            [jnp.dot(pb[h * hrows:(h + 1) * hrows], vnp_ref[:, h * HEAD_W:(h + 1) * HEAD_W],
                     preferred_element_type=F32) for h in range(A_HEADS)], axis=0)

    kcat = jnp.concatenate(
        [r[...].reshape(BRANCH_W, PAGE_SIZE).astype(BF16) for r in kc_refs], axis=1)
    s = jnp.dot(qbd_ref[...], kcat, preferred_element_type=F32)
    alpha, pb = stats(s)
    parts = []
    for h in range(A_HEADS):
        vh = jnp.concatenate(
            [r[pl.ds(h, PAGE_SIZE, stride=A_HEADS), :].astype(BF16) for r in vc_refs], axis=0)
        parts.append(jnp.dot(pb[h * hrows:(h + 1) * hrows], vh, preferred_element_type=F32))
    acc_ref[...] = alpha * acc_ref[...] + jnp.concatenate(parts, axis=0)

    @pl.when(p_idx == pl.num_programs(1) - 1)
    def _():
        o = acc_ref[...] / jnp.sum(l_ref[...], axis=-1, keepdims=True)
        lam = _lambda(lam_ref, lam_init)
        subg = subg_ref[...]
        for h in range(A_HEADS):
            o1 = o[(2 * h) * t:(2 * h + 1) * t]
            o2 = o[(2 * h + 1) * t:(2 * h + 2) * t]
            zh = z_ref[:, h * HEAD_W:(h + 1) * HEAD_W]
            out = _diff_head_out(o1, o2, lam, subg, zh, lam_init)
            o_ref[:, h * HEAD_W:(h + 1) * HEAD_W] = out.astype(o_ref.dtype)


def _attn_sample(q_bf, k_bf, v_bf, cache_k, cache_v, layer, page_table, proj, subg, lam_vecs,
                 lam_init, *, n_seq, t, pps=16):
    n_pages = page_table.shape[1]
    rows = 2 * A_HEADS * t
    pt_flat = page_table.reshape(-1)
    kc = jnp.transpose(cache_k, (0, 1, 3, 4, 2))
    vc = cache_v.reshape(cache_v.shape[0], cache_v.shape[1], PAGE_SIZE * A_HEADS, HEAD_W)
    seq_spec = pl.BlockSpec((t, BRANCH_W), lambda b, p, pt: (b, 0))

    def page(i, ndim=3):
        return lambda b, p, pt: (layer, pt[b * n_pages + p * pps + i]) + (0,) * ndim

    k_specs = [pl.BlockSpec((None, None, 2 * A_HEADS, A_HD, PAGE_SIZE), page(i))
               for i in range(pps)]
    v_specs = [pl.BlockSpec((None, None, PAGE_SIZE * A_HEADS, HEAD_W), page(i, 2))
               for i in range(pps)]
    grid_spec = pltpu.PrefetchScalarGridSpec(
        num_scalar_prefetch=1,
        grid=(n_seq, n_pages // pps),
        in_specs=[seq_spec, seq_spec, seq_spec] + k_specs + v_specs + [
            pl.BlockSpec((t, BRANCH_W), lambda b, p, pt: (b, COL_Z)),
            pl.BlockSpec((1, HEAD_W), lambda b, p, pt: (0, 0)),
            pl.BlockSpec((4, A_HD), lambda b, p, pt: (0, 0))],
        out_specs=pl.BlockSpec((t, BRANCH_W), lambda b, p, pt: (b, 0)),
        scratch_shapes=[pltpu.VMEM((rows, BRANCH_W), BF16),
                        pltpu.VMEM((PAGE_SIZE, BRANCH_W), BF16),
                        pltpu.VMEM((PAGE_SIZE, BRANCH_W), BF16),
                        pltpu.VMEM((rows, LANES), F32), pltpu.VMEM((rows, LANES), F32),
                        pltpu.VMEM((rows, HEAD_W), F32)],
    )
    return pl.pallas_call(
        functools.partial(_attn_s_kernel, t=t, pps=pps, lam_init=lam_init),
        out_shape=jax.ShapeDtypeStruct((n_seq * t, BRANCH_W), BF16),
        grid_spec=grid_spec,
        compiler_params=_params("arbitrary", "arbitrary"),
        name="attn_sample",
    )(pt_flat, q_bf, k_bf, v_bf, *([kc] * pps), *([vc] * pps), proj, subg, lam_vecs)


def _window_sum(e, log2_win):
    s = e
    for k in range(log2_win):
        s = s + pltpu.roll(s, 1 << k, 0)
    return s


def _pool_groups(ext, pos, w_ref, scale_ref, z, o_ref, pick):
    for gi, win in enumerate(POOL_WINDOWS):
        c0, c1 = gi * POOL_GW, (gi + 1) * POOL_GW
        e = ext[:, c0:c1]
        cur = pick(e)
        tot = pick(_window_sum(e, gi + 1))
        cnt = jnp.minimum(pos + 1, win).astype(F32)
        mixed = tot / cnt - cur
        out = jnp.dot(mixed.astype(BF16), w_ref[gi].astype(BF16), preferred_element_type=F32)
        out = out * scale_ref[:, c0:c1]
        o_ref[:, c0:c1] = (out * _silu(z[:, c0:c1])).astype(o_ref.dtype)


def _pool_p_kernel(u_ref, prev_ref, z_ref, w_ref, scale_ref, o_ref, *, tm, tps):
    i = pl.program_id(0)
    first = (i % tps) == 0
    prev = jnp.where(first, 0.0, prev_ref[...])
    ext = jnp.concatenate([prev, u_ref[...]], axis=0)
    pos = (i % tps) * tm + lax.broadcasted_iota(jnp.int32, (tm, POOL_GW), 0)
    _pool_groups(ext, pos, w_ref, scale_ref, z_ref[...], o_ref, lambda x: x[POOL_PAD:])


def _pool_s_kernel(u_ref, buf_ref, z_ref, w_ref, scale_ref, o_ref, *, nb, t, past_len):
    u3 = u_ref[...].reshape(nb, t, BRANCH_W)
    ext = jnp.concatenate([buf_ref[...], u3], axis=1)
    ext = ext.reshape(nb * (POOL_PAD + t), BRANCH_W)
    pos = past_len + lax.broadcasted_iota(jnp.int32, (nb * t, POOL_GW), 0) % t

    def pick(x):
        x3 = x.reshape(nb, POOL_PAD + t, x.shape[-1])
        return x3[:, POOL_PAD:].reshape(nb * t, x.shape[-1])

    _pool_groups(ext, pos, w_ref, scale_ref, z_ref[...], o_ref, pick)


def _pool_prompt(proj, pool_w, pool_scale, layer, *, seq_len, tm=512):
    m = proj.shape[0]
    tps = seq_len // tm
    per = tm // POOL_PAD
    return pl.pallas_call(
        functools.partial(_pool_p_kernel, tm=tm, tps=tps),
        out_shape=jax.ShapeDtypeStruct((m, BRANCH_W), BF16),
        grid=(m // tm,),
        in_specs=[pl.BlockSpec((tm, BRANCH_W), lambda i: (i, COL_U)),
                  pl.BlockSpec((POOL_PAD, BRANCH_W),
                               lambda i: (jnp.maximum(i * per - 1, 0), COL_U)),
                  pl.BlockSpec((tm, BRANCH_W), lambda i: (i, COL_Z + 1)),
                  pl.BlockSpec((None,) + pool_w.shape[1:], lambda i: (layer, 0, 0, 0)),
                  pl.BlockSpec((1, BRANCH_W), lambda i: (0, 0))],
        out_specs=pl.BlockSpec((tm, BRANCH_W), lambda i: (i, 0)),
        compiler_params=_params("parallel"),
        name="pool_prompt",
    )(proj, proj, proj, pool_w, pool_scale.reshape(1, BRANCH_W))


def _pool_sample(proj, buf_padded, pool_w, pool_scale, layer, *, t, past_len, nb=32):
    m = proj.shape[0]
    return pl.pallas_call(
        functools.partial(_pool_s_kernel, nb=nb, t=t, past_len=past_len),
        out_shape=jax.ShapeDtypeStruct((m, BRANCH_W), BF16),
        grid=(m // (nb * t),),
        in_specs=[pl.BlockSpec((nb * t, BRANCH_W), lambda i: (i, COL_U)),
                  pl.BlockSpec((nb, POOL_PAD, BRANCH_W), lambda i: (i, 0, 0)),
                  pl.BlockSpec((nb * t, BRANCH_W), lambda i: (i, COL_Z + 1)),
                  pl.BlockSpec((None,) + pool_w.shape[1:], lambda i: (layer, 0, 0, 0)),
                  pl.BlockSpec((1, BRANCH_W), lambda i: (0, 0))],
        out_specs=pl.BlockSpec((nb * t, BRANCH_W), lambda i: (i, 0)),
        compiler_params=_params("parallel"),
        name="pool_sample",
    )(proj, buf_padded, proj, pool_w, pool_scale.reshape(1, BRANCH_W))


def _conv_taps(ext, pick, cw_ref, cb_ref):
    out = cb_ref[...] + pick(ext) * cw_ref[CONV_W - 1:CONV_W, :]
    for k in range(CONV_W - 1):
        shift = CONV_W - 1 - k
        out = out + pick(pltpu.roll(ext, shift, 0)) * cw_ref[k:k + 1, :]
    return out


def _block_diag_gate(xb, w_ref, b_ref):
    parts = [jnp.dot(xb[:, n * LRU_BW:(n + 1) * LRU_BW], w_ref[n].astype(BF16),
                     preferred_element_type=F32) for n in range(LRU_BLOCKS)]
    return jax.nn.sigmoid(jnp.concatenate(parts, axis=-1) + b_ref[...])


def _lru_coeffs(xc, wa_ref, ba_ref, wx_ref, bx_ref, lam_ref):
    xb = xc.astype(BF16)
    r = _block_diag_gate(xb, wa_ref, ba_ref)
    i = _block_diag_gate(xb, wx_ref, bx_ref)
    nl = -lam_ref[...]
    softplus = jnp.maximum(nl, 0.0) + jnp.log1p(jnp.exp(-jnp.abs(nl)))
    log_a = r * (-LRU_C * softplus)
    a = jnp.exp(log_a)
    b = jnp.sqrt(1.0 - a * a) * (i * xc)
    return a, b


def _group_scan(a, b):
    shape = a.shape
    grouped = (shape[0] // SUBLANES, SUBLANES, shape[1])
    a = a.reshape(grouped)
    b = b.reshape(grouped)
    row = lax.broadcasted_iota(jnp.int32, grouped, 1)
    d = 1
    while d < SUBLANES:
        valid = row >= d
        a_sh = pltpu.roll(a, d, 1)
        b_sh = pltpu.roll(b, d, 1)
        b = jnp.where(valid, a * b_sh + b, b)
        a = jnp.where(valid, a * a_sh, a)
        d *= 2
    return a.reshape(shape), b.reshape(shape)


def _lru_p_kernel(x_ref, prev_ref, z_ref, cw_ref, cb_ref, wa_ref, ba_ref, wx_ref, bx_ref,
                  lam_ref, o_ref, hl_ref, a_s, b_s, h_s, *, tm, tps):
    i = pl.program_id(0)
    first = (i % tps) == 0

    @pl.when(first)
    def _():
        h_s[...] = jnp.zeros(h_s.shape, F32)

    prev = jnp.where(first, 0.0, prev_ref[...])
    ext = jnp.concatenate([prev, x_ref[...]], axis=0)
    xc = _conv_taps(ext, lambda x: x[CONV_PAD:], cw_ref, cb_ref)
    a, b = _lru_coeffs(xc, wa_ref, ba_ref, wx_ref, bx_ref, lam_ref)
    a, b = _group_scan(a, b)
    a_s[...] = a
    b_s[...] = b

    def body(g, h):
        r0 = pl.multiple_of(g * SUBLANES, SUBLANES)
        hs = a_s[pl.ds(r0, SUBLANES), :] * h + b_s[pl.ds(r0, SUBLANES), :]
        b_s[pl.ds(r0, SUBLANES), :] = hs
        return hs[SUBLANES - 1:SUBLANES, :]

    h = lax.fori_loop(0, tm // SUBLANES, body, h_s[...])
    h_s[...] = h
    hl_ref[...] = h
    o_ref[...] = (b_s[...] * _silu(z_ref[...])).astype(o_ref.dtype)


def _lru_s_kernel(x_ref, buf_ref, h0_ref, z_ref, cw_ref, cb_ref, wa_ref, ba_ref, wx_ref, bx_ref,
                  lam_ref, o_ref, hs_ref, *, nb, t):
    x3 = x_ref[...].reshape(nb, t, BRANCH_W)
    ext = jnp.concatenate([buf_ref[...], x3], axis=1).reshape(nb * (CONV_PAD + t), BRANCH_W)

    def pick(x):
        return x.reshape(nb, CONV_PAD + t, BRANCH_W)[:, CONV_PAD:].reshape(nb * t, BRANCH_W)

    xc = _conv_taps(ext, pick, cw_ref, cb_ref)
    a, b = _lru_coeffs(xc, wa_ref, ba_ref, wx_ref, bx_ref, lam_ref)
    a, b = _group_scan(a, b)
    hs = a * h0_ref[...] + b
    hs_ref[...] = hs
    o_ref[...] = (hs * _silu(z_ref[...])).astype(o_ref.dtype)


def _lru_weight_specs():
    full = lambda shape: pl.BlockSpec(shape, lambda *a: (0,) * len(shape))
    return [full((CONV_W, BRANCH_W)), full((1, BRANCH_W)),
            full((LRU_BLOCKS, LRU_BW, LRU_BW)), full((1, BRANCH_W)),
            full((LRU_BLOCKS, LRU_BW, LRU_BW)), full((1, BRANCH_W)),
            full((1, BRANCH_W))]


def _lru_weights(conv_w, conv_b, wa, ba, wx, bx, lam):
    r = lambda v: v.reshape(1, BRANCH_W)
    return (conv_w, r(conv_b), wa, r(ba), wx, r(bx), r(lam))


def _lru_prompt(proj, weights, *, batch, seq_len, tm=512):
    m = proj.shape[0]
    tps = seq_len // tm
    per = tm // CONV_PAD
    return pl.pallas_call(
        functools.partial(_lru_p_kernel, tm=tm, tps=tps),
        out_shape=(jax.ShapeDtypeStruct((m, BRANCH_W), BF16),
                   jax.ShapeDtypeStruct((batch, 1, BRANCH_W), F32)),
        grid=(m // tm,),
        in_specs=[pl.BlockSpec((tm, BRANCH_W), lambda i: (i, COL_X)),
                  pl.BlockSpec((CONV_PAD, BRANCH_W),
                               lambda i: (jnp.maximum(i * per - 1, 0), COL_X)),
                  pl.BlockSpec((tm, BRANCH_W), lambda i: (i, COL_Z + 2))]
        + _lru_weight_specs(),
        out_specs=(pl.BlockSpec((tm, BRANCH_W), lambda i: (i, 0)),
                   pl.BlockSpec((None, 1, BRANCH_W), lambda i: (i // tps, 0, 0))),
        scratch_shapes=[pltpu.VMEM((tm, BRANCH_W), F32), pltpu.VMEM((tm, BRANCH_W), F32),
                        pltpu.VMEM((1, BRANCH_W), F32)],
        compiler_params=_params("arbitrary"),
        name="lru_prompt",
    )(proj, proj, proj, *weights)


def _lru_sample(proj, buf_padded, h0_rows, weights, *, t, nb=32):
    m = proj.shape[0]
    rows = nb * t
    return pl.pallas_call(
        functools.partial(_lru_s_kernel, nb=nb, t=t),
        out_shape=(jax.ShapeDtypeStruct((m, BRANCH_W), BF16),
                   jax.ShapeDtypeStruct((m, BRANCH_W), F32)),
        grid=(m // rows,),
        in_specs=[pl.BlockSpec((rows, BRANCH_W), lambda i: (i, COL_X)),
                  pl.BlockSpec((nb, CONV_PAD, BRANCH_W), lambda i: (i, 0, 0)),
                  pl.BlockSpec((rows, BRANCH_W), lambda i: (i, 0)),
                  pl.BlockSpec((rows, BRANCH_W), lambda i: (i, COL_Z + 2))]
        + _lru_weight_specs(),
        out_specs=(pl.BlockSpec((rows, BRANCH_W), lambda i: (i, 0)),
                   pl.BlockSpec((rows, BRANCH_W), lambda i: (i, 0))),
        compiler_params=_params("parallel"),
        name="lru_sample",
    )(proj, buf_padded, h0_rows, proj, *weights)


def _mem_attn_kernel(q_ref, k_ref, v_ref, z_ref, o_ref):
    q = q_ref[...].astype(BF16)
    k = k_ref[...].astype(BF16)
    v = v_ref[...].astype(BF16)
    z = z_ref[...]
    for h in range(M_HEADS):
        c0, c1 = h * M_HD, (h + 1) * M_HD
        s = lax.dot_general(q[:, c0:c1], k[:, c0:c1], _NT, preferred_element_type=F32)
        s = s * (M_HD ** -0.5)
        p = jnp.exp(s - jnp.max(s, axis=-1, keepdims=True))
        den = jnp.sum(p, axis=-1, keepdims=True)
        o = jnp.dot((p / den).astype(BF16), v[:, c0:c1], preferred_element_type=F32)
        o_ref[:, c0:c1] = (o * _silu(z[:, c0:c1])).astype(o_ref.dtype)


def _mem_attn(proj, mem_k, mem_v, layer, *, n_seq, rows_per_seq, tq):
    n_mem = mem_k.shape[1] // n_seq
    nq = rows_per_seq // tq
    kv_spec = pl.BlockSpec((None, n_mem, BRANCH_W), lambda b, qi: (layer, b, 0))
    return pl.pallas_call(
        _mem_attn_kernel,
        out_shape=jax.ShapeDtypeStruct((n_seq * rows_per_seq, BRANCH_W), BF16),
        grid=(n_seq, nq),
        in_specs=[pl.BlockSpec((tq, BRANCH_W), lambda b, qi: (b * nq + qi, COL_QD)),
                  kv_spec, kv_spec,
                  pl.BlockSpec((tq, BRANCH_W), lambda b, qi: (b * nq + qi, COL_Z + 3))],
        out_specs=pl.BlockSpec((tq, BRANCH_W), lambda b, qi: (b * nq + qi, 0)),
        compiler_params=_params("parallel", "parallel"),
        name="mem_attn",
    )(proj, mem_k, mem_v, proj)


def _mem_attn_s_kernel(q_ref, k_ref, v_ref, z_ref, o_ref, *, nb, t):
    n_mem = k_ref.shape[1]
    rows = M_HEADS * t
    col = lax.broadcasted_iota(jnp.int32, (rows, n_mem * M_HEADS), 1)
    row = lax.broadcasted_iota(jnp.int32, (rows, n_mem * M_HEADS), 0)
    own_head = (col % M_HEADS) == (row // t)
    for b in range(nb):
        q = q_ref[b * t:(b + 1) * t, :].astype(BF16)
        qs = jnp.concatenate([q[:, h * M_HD:(h + 1) * M_HD] for h in range(M_HEADS)], axis=0)
        k2 = k_ref[b].reshape(n_mem * M_HEADS, M_HD).astype(BF16)
        v2 = v_ref[b].reshape(n_mem * M_HEADS, M_HD).astype(BF16)
        s = lax.dot_general(qs, k2, _NT, preferred_element_type=F32) * (M_HD ** -0.5)
        s = jnp.where(own_head, s, NEG)
        p = jnp.exp(s - jnp.max(s, axis=-1, keepdims=True))
        den = jnp.sum(p, axis=-1, keepdims=True)
        o = jnp.dot((p / den).astype(BF16), v2, preferred_element_type=F32)
        for h in range(M_HEADS):
            c0, c1 = h * M_HD, (h + 1) * M_HD
            zh = z_ref[b * t:(b + 1) * t, c0:c1]
            o_ref[b * t:(b + 1) * t, c0:c1] = (o[h * t:(h + 1) * t] * _silu(zh)).astype(o_ref.dtype)


def _mem_attn_sample(proj, cache_mem_k, cache_mem_v, layer, *, t, nb=8):
    _, n_seq, n_mem, _, _ = cache_mem_k.shape
    kv_spec = pl.BlockSpec((None, nb, n_mem, M_HEADS, M_HD), lambda i: (layer, i, 0, 0, 0))
    return pl.pallas_call(
        functools.partial(_mem_attn_s_kernel, nb=nb, t=t),
        out_shape=jax.ShapeDtypeStruct((n_seq * t, BRANCH_W), BF16),
        grid=(n_seq // nb,),
        in_specs=[pl.BlockSpec((nb * t, BRANCH_W), lambda i: (i, COL_QD)),
                  kv_spec, kv_spec,
                  pl.BlockSpec((nb * t, BRANCH_W), lambda i: (i, COL_Z + 3))],
        out_specs=pl.BlockSpec((nb * t, BRANCH_W), lambda i: (i, 0)),
        compiler_params=_params("parallel"),
        name="mem_attn_sample",
    )(proj, cache_mem_k, cache_mem_v, proj)


def _merge_kernel(*refs):
    h_refs, g_refs = refs[:N_BRANCH], refs[N_BRANCH:2 * N_BRANCH]
    w_ref, o_ref = refs[2 * N_BRANCH:]
    acc = None
    for n in range(N_BRANCH):
        up = jnp.dot(h_refs[n][...], w_ref[n], preferred_element_type=F32)
        term = jax.nn.sigmoid(g_refs[n][...]) * up
        acc = term if acc is None else acc + term
    o_ref[...] = acc.astype(o_ref.dtype)


def _merge(h_a, h_b, h_c, h_d, proj, w_branch_bf, layer, tm=256, tn=D_MODEL):
    m = h_a.shape[0]
    tm = min(tm, m)
    h_spec = pl.BlockSpec((tm, BRANCH_W), lambda i, j: (i, 0))
    gate_col0 = COL_G * BRANCH_W // tn
    g_specs = [pl.BlockSpec((tm, tn), lambda i, j, n=n: (i, gate_col0 + n * (D_MODEL // tn) + j))
               for n in range(N_BRANCH)]
    return pl.pallas_call(
        _merge_kernel,
        out_shape=jax.ShapeDtypeStruct((m, D_MODEL), BF16),
        grid=(m // tm, D_MODEL // tn),
        in_specs=[h_spec] * N_BRANCH + g_specs + [
            pl.BlockSpec((None, N_BRANCH, BRANCH_W, tn), lambda i, j: (layer, 0, 0, j),
                         pipeline_mode=pl.Buffered(1))],
        out_specs=pl.BlockSpec((tm, tn), lambda i, j: (i, j)),
        compiler_params=_params("parallel", "parallel"),
        name="merge",
    )(h_a, h_b, h_c, h_d, proj, proj, proj, proj, w_branch_bf)


def _out_norm_kernel(a_ref, w_ref, r_ref, g_ref, *rest, want_x, eps):
    if want_x:
        x_ref, n_ref, wb_ref = rest
    else:
        n_ref, wb_ref = rest

    @pl.when(pl.program_id(0) == 0)
    def _():
        wb_ref[...] = w_ref[...].astype(BF16)

    x = r_ref[...] + jnp.dot(a_ref[...], wb_ref[...], preferred_element_type=F32)
    if want_x:
        x_ref[...] = x
    y = x * lax.rsqrt(jnp.mean(x * x, axis=-1, keepdims=True) + eps)
    n_ref[...] = (y * g_ref[...]).astype(n_ref.dtype)


def _out_proj_norm(a, w, layer, res, g, *, want_x, norm_dtype, tm=512):
    m, k = a.shape
    n = w.shape[2]
    tm = min(tm, m)
    row = lambda width: pl.BlockSpec((tm, width), lambda i: (i, 0))
    out_shape = [jax.ShapeDtypeStruct((m, n), norm_dtype)]
    out_specs = [row(n)]
    if want_x:
        out_shape.insert(0, jax.ShapeDtypeStruct((m, n), F32))
        out_specs.insert(0, row(n))
    outs = pl.pallas_call(
        functools.partial(_out_norm_kernel, want_x=want_x, eps=EPS),
        out_shape=tuple(out_shape),
        grid=(m // tm,),
        in_specs=[row(k),
                  pl.BlockSpec((None, k, n), lambda i: (layer, 0, 0),
                               pipeline_mode=pl.Buffered(1)),
                  row(n),
                  pl.BlockSpec((1, n), lambda i: (0, 0))],
        out_specs=tuple(out_specs),
        scratch_shapes=[pltpu.VMEM((k, n), BF16)],
        compiler_params=_params("arbitrary"),
        name="out_proj_norm",
    )(a, w, res, g.reshape(1, n))
    return outs if want_x else (None, outs[0])


def _layer_dense_out(x, h_a, h_b, h_c, h_d, proj, w_branch_bf, w_out, layer, g_next, last):
    merged = _merge(h_a, h_b, h_c, h_d, proj, w_branch_bf, layer)
    return _out_proj_norm(merged, w_out, layer, x, g_next, want_x=not last,
                          norm_dtype=F32 if last else BF16)


def kernel(x_prompt, x_sample, cache_k, cache_v, cache_mem_k, cache_mem_v, state_pool, state_conv, state_h, page_table, mem_prompt, ln_g, w_in, lam_q1, lam_k1, lam_q2, lam_k2, attn_sub_g, pool_w, pool_scale, conv_w, conv_b, lru_wa, lru_ba, lru_wx, lru_bx, lru_lambda, mem_g, w_mem_k, w_mem_v, w_branch, w_out, final_g):
    batch, seq_len, _ = x_prompt.shape
    n_seq, t, _ = x_sample.shape
    depth = w_in.shape[0]
    n_mem = mem_prompt.shape[1]
    past_len = page_table.shape[1] * PAGE_SIZE
    rope_tm = 256

    cos_p, sin_p = _rope_tables(jnp.arange(seq_len, dtype=jnp.int32))
    cos_s, sin_s = _rope_tables(past_len + jnp.arange(t, dtype=jnp.int32))
    cos_s = jnp.tile(cos_s, (rope_tm // t, 1))
    sin_s = jnp.tile(sin_s, (rope_tm // t, 1))

    xp = x_prompt.reshape(batch * seq_len, D_MODEL)
    xs = x_sample.reshape(n_seq * t, D_MODEL)
    mem_rows = mem_prompt.reshape(batch * n_mem, D_MODEL)

    w_branch_bf = w_branch.astype(BF16)

    outs = {name: [] for name in ("kp", "vp", "ks", "vs", "mk", "mv", "pp", "ps", "cp", "cs", "hp", "hs")}
    xn_p = _rmsnorm(xp, ln_g[0], EPS, BF16)
    xn_s = _rmsnorm(xs, ln_g[0], EPS, BF16)
    for l in range(depth):
        last = l == depth - 1
        g_next = final_g if last else ln_g[l + 1]
        lam_init = 0.8 - 0.6 * math.exp(-0.3 * l)
        lam_vecs = jnp.stack([lam_q1[l], lam_k1[l], lam_q2[l], lam_k2[l]])
        subg = attn_sub_g[l].reshape(1, HEAD_W)
        lru_w = _lru_weights(conv_w[l], conv_b[l], lru_wa[l], lru_ba[l], lru_wx[l], lru_bx[l],
                             lru_lambda[l])

        mn = _rmsnorm(mem_rows, mem_g[l], EPS, BF16)
        mk = _matmul(mn, w_mem_k, l)
        mv = _matmul(mn, w_mem_v, l)
        proj = _matmul(xn_p, w_in, l, tn=1536)
        k_f, v_f, k_bf, v_bf, q_st = _rope_call(proj, cos_p, sin_p, seq_len=seq_len, tm=rope_tm)
        h_a = _attn_prompt(q_st, k_bf, v_bf, proj, subg, lam_vecs, lam_init,
                           batch=batch, seq_len=seq_len)
        h_b = _pool_prompt(proj, pool_w, pool_scale[l], l, seq_len=seq_len)
        h_c, h_last = _lru_prompt(proj, lru_w, batch=batch, seq_len=seq_len)
        h_d = _mem_attn(proj, mk[None], mv[None], 0, n_seq=batch, rows_per_seq=seq_len, tq=1024)
        proj3 = proj.reshape(batch, seq_len, -1)
        outs["pp"].append(proj3[:, seq_len - POOL_BUF:, COL_U * BRANCH_W:(COL_U + 1) * BRANCH_W])
        outs["cp"].append(proj3[:, seq_len - (CONV_W - 1):, COL_X * BRANCH_W:(COL_X + 1) * BRANCH_W])
        xp, xn_p = _layer_dense_out(xp, h_a, h_b, h_c, h_d, proj, w_branch_bf, w_out, l,
                                    g_next, last)
        outs["kp"].append(k_f.reshape(batch, 2 * A_HEADS, A_HD, seq_len))
        outs["vp"].append(v_f.reshape(batch, seq_len, A_HEADS, HEAD_W))
        outs["mk"].append(mk.reshape(batch, n_mem, M_HEADS, M_HD))
        outs["mv"].append(mv.reshape(batch, n_mem, M_HEADS, M_HD))
        outs["hp"].append(h_last.reshape(batch, BRANCH_W))

        proj = _matmul(xn_s, w_in, l, tn=1536)
        k_f, v_f, k_bf, v_bf, q_bf = _rope_call(proj, cos_s, sin_s, seq_len=0, tm=rope_tm)
        h_a = _attn_sample(q_bf, k_bf, v_bf, cache_k, cache_v, l, page_table, proj, subg,
                           lam_vecs, lam_init, n_seq=n_seq, t=t)
        pool_buf = jnp.pad(state_pool[l], ((0, 0), (POOL_PAD - POOL_BUF, 0), (0, 0)))
        h_b = _pool_sample(proj, pool_buf, pool_w, pool_scale[l], l, t=t, past_len=past_len)
        conv_buf = jnp.pad(state_conv[l], ((0, 0), (CONV_PAD - (CONV_W - 1), 0), (0, 0)))
        h0_rows = jnp.repeat(state_h[l], t, axis=0)
        h_c, hs_rows = _lru_sample(proj, conv_buf, h0_rows, lru_w, t=t)
        h_d = _mem_attn_sample(proj, cache_mem_k, cache_mem_v, l, t=t)
        proj3 = proj.reshape(n_seq, t, -1)
        u_s = proj3[:, :, COL_U * BRANCH_W:(COL_U + 1) * BRANCH_W]
        x_s = proj3[:, :, COL_X * BRANCH_W:(COL_X + 1) * BRANCH_W]
        outs["ps"].append(jnp.concatenate([state_pool[l], u_s], axis=1)[:, -POOL_BUF:])
        outs["cs"].append(jnp.concatenate([state_conv[l], x_s], axis=1)[:, -(CONV_W - 1):])
        xs, xn_s = _layer_dense_out(xs, h_a, h_b, h_c, h_d, proj, w_branch_bf, w_out, l,
                                    g_next, last)
        outs["ks"].append(k_f.reshape(n_seq, t, 2 * A_HEADS, A_HD))
        outs["vs"].append(v_f.reshape(n_seq, t, A_HEADS, HEAD_W))
        outs["hs"].append(hs_rows.reshape(n_seq, t, BRANCH_W)[:, t - 1])

    y_prompt = xn_p.reshape(batch, seq_len, D_MODEL)
    y_sample = xn_s.reshape(n_seq, t, D_MODEL)
    st = lambda name: jnp.stack(outs[name])
    k_prompt = jnp.transpose(st("kp"), (0, 1, 4, 2, 3))
    return (y_prompt, y_sample, k_prompt, st("vp"), st("ks"), st("vs"), st("mk"), st("mv"),
            st("pp"), st("ps"), st("cp"), st("cs"), st("hp"), st("hs"))
```
